```python
import math
import jax, jax.numpy as jnp
from jax import lax
import numpy as np

D_MODEL = 1024
BATCH = 8
SEQ = 8192
DEPTH = 1

D_MIX = 2 * D_MODEL
SSD_WIDTH = D_MIX // 2
SSD_HEAD_DIM = 64
SSD_HEADS = SSD_WIDTH // SSD_HEAD_DIM
SSD_GROUPS = 2
SSD_HPG = SSD_HEADS // SSD_GROUPS
SSD_STATE = 128
CONV_WIDTH = 4
RET_WIDTH = D_MIX - SSD_WIDTH
RET_HEADS = 8
RET_V_DIM = RET_WIDTH // RET_HEADS
RET_QK_DIM = RET_V_DIM // 2
CHUNK = 128
D_FF = -(-8 * D_MODEL // (3 * 256)) * 256
ROPE_BASE = 10000.0
EPS = 1e-6
DT_MIN = 0.001
DT_MAX = 0.1

BC_WIDTH = SSD_GROUPS * SSD_STATE
CONV_CH = SSD_WIDTH + 2 * BC_WIDTH
QK_WIDTH = RET_HEADS * RET_QK_DIM
PROJ_SIZES = [SSD_WIDTH, CONV_CH, SSD_HEADS, QK_WIDTH, QK_WIDTH, RET_WIDTH, RET_WIDTH]
PROJ_SPLITS = [int(s) for s in np.cumsum(PROJ_SIZES)[:-1]]
D_PROJ = sum(PROJ_SIZES)

kernel_name = "hybrid_ssd_retention_parallel_heads"


def rmsnorm(x, w):
    xf = x.astype(jnp.float32)
    y = xf * lax.rsqrt(jnp.mean(xf * xf, axis=-1, keepdims=True) + EPS)
    return (y * w.astype(jnp.float32)).astype(x.dtype)


def causal_depthwise_conv(u, w, b):
    ch = u.shape[-1]
    out = lax.conv_general_dilated(
        u, w[:, None, :].astype(u.dtype), window_strides=(1,),
        padding=[(CONV_WIDTH - 1, 0)],
        dimension_numbers=("NWC", "WIO", "NWC"),
        feature_group_count=ch)
    return out + b.astype(u.dtype)


def chunk_state_scan(states, decay):
    def step(carry, inp):
        s, d = inp
        return (carry * d + s).astype(carry.dtype), carry
    _, prev = lax.scan(step, jnp.zeros_like(states[0]), (states, decay))
    return prev


def ssd_mixer(z, xbc, dt_raw, conv_w, conv_b, dt_bias, a_log, d_skip, norm_w):
    bsz, seqlen, _ = z.shape
    nc = seqlen // CHUNK
    xbc = jax.nn.silu(causal_depthwise_conv(xbc, conv_w, conv_b))
    xs, bm, cm = jnp.split(xbc, [SSD_WIDTH, SSD_WIDTH + BC_WIDTH], axis=-1)
    x_c = xs.reshape(bsz, nc, CHUNK, SSD_GROUPS, SSD_HPG, SSD_HEAD_DIM)
    b_c = bm.reshape(bsz, nc, CHUNK, SSD_GROUPS, SSD_STATE)
    c_c = cm.reshape(bsz, nc, CHUNK, SSD_GROUPS, SSD_STATE)
    dt = jax.nn.softplus(dt_raw.astype(jnp.float32) + dt_bias.astype(jnp.float32))
    dt_c = dt.reshape(bsz, nc, CHUNK, SSD_GROUPS, SSD_HPG)
    a = -jnp.exp(a_log.astype(jnp.float32)).reshape(SSD_GROUPS, SSD_HPG)
    a_cs = jnp.cumsum(dt_c * a, axis=2)
    xdt = x_c.astype(jnp.float32) * dt_c[..., None]

    a_cs_t = jnp.moveaxis(a_cs, 2, -1)
    seg = a_cs_t[..., :, None] - a_cs_t[..., None, :]
    mask = jnp.tril(jnp.ones((CHUNK, CHUNK), dtype=bool))
    decay = jnp.exp(jnp.where(mask, seg, -jnp.inf))
    cb = jnp.einsum("bclgn,bcsgn->bcgls", c_c, b_c).astype(jnp.float32)
    y_diag = jnp.einsum("bcghls,bcsghp->bclghp", cb[:, :, :, None] * decay, xdt)

    decay_to_end = jnp.exp(a_cs[:, :, -1:] - a_cs)
    states = jnp.einsum("bclgn,bclgh,bclghp->bcghpn",
                        b_c.astype(jnp.float32), decay_to_end, xdt)
    chunk_decay = jnp.exp(a_cs[:, :, -1])
    prev = chunk_state_scan(jnp.moveaxis(states, 1, 0),
                            jnp.moveaxis(chunk_decay, 1, 0)[..., None, None])
    prev = jnp.moveaxis(prev, 0, 1)
    y_off = jnp.einsum("bclgn,bcghpn,bclgh->bclghp",
                       c_c.astype(jnp.float32), prev, jnp.exp(a_cs))

    d = d_skip.astype(jnp.float32).reshape(SSD_GROUPS, SSD_HPG, 1)
    y = y_diag + y_off + x_c.astype(jnp.float32) * d
    y = y.reshape(bsz, seqlen, SSD_WIDTH) * jax.nn.silu(z.astype(jnp.float32))
    yg = y.reshape(bsz, seqlen, SSD_GROUPS, SSD_WIDTH // SSD_GROUPS)
    yg = yg * lax.rsqrt(jnp.mean(yg * yg, axis=-1, keepdims=True) + EPS)
    y = yg.reshape(bsz, seqlen, SSD_WIDTH) * norm_w.astype(jnp.float32)
    return y.astype(z.dtype)


def rotary(t, positions):
    half = t.shape[-1] // 2
    inv_freq = ROPE_BASE ** (-jnp.arange(half, dtype=jnp.float32) / half)
    ang = positions[:, None] * inv_freq[None, :]
    cos = jnp.cos(ang)[:, None, :]
    sin = jnp.sin(ang)[:, None, :]
    tf = t.astype(jnp.float32)
    t1, t2 = tf[..., :half], tf[..., half:]
    return jnp.concatenate([t1 * cos - t2 * sin, t1 * sin + t2 * cos], axis=-1)


def retention_mixer(q, k, v, g, norm_w):
    bsz, seqlen, _ = q.shape
    nc = seqlen // CHUNK
    positions = jnp.arange(seqlen, dtype=jnp.float32)
    q = rotary(q.reshape(bsz, seqlen, RET_HEADS, RET_QK_DIM), positions)
    k = rotary(k.reshape(bsz, seqlen, RET_HEADS, RET_QK_DIM), positions) * (RET_QK_DIM ** -0.5)
    v = v.astype(jnp.float32).reshape(bsz, seqlen, RET_HEADS, RET_V_DIM)
    qc = q.reshape(bsz, nc, CHUNK, RET_HEADS, RET_QK_DIM)
    kc = k.reshape(bsz, nc, CHUNK, RET_HEADS, RET_QK_DIM)
    vc = v.reshape(bsz, nc, CHUNK, RET_HEADS, RET_V_DIM)

    log_gamma = jnp.log1p(-jnp.exp2(-5.0 - jnp.arange(RET_HEADS, dtype=jnp.float32)))
    pos = jnp.arange(CHUNK, dtype=jnp.float32)
    diff = pos[:, None] - pos[None, :]
    intra_decay = jnp.where(diff[None] >= 0,
                            jnp.exp(jnp.maximum(diff, 0.0)[None] * log_gamma[:, None, None]),
                            0.0)

    scores = jnp.einsum("bclhd,bcshd->bchls", qc, kc) * intra_decay
    y_intra = jnp.einsum("bchls,bcshv->bclhv", scores, vc)

    k_decay = jnp.exp((CHUNK - 1 - pos)[:, None] * log_gamma[None, :])
    states = jnp.einsum("bclhd,lh,bclhv->bchdv", kc, k_decay, vc)
    chunk_decay = jnp.exp(CHUNK * log_gamma)[None, None, :, None, None]
    chunk_decay = jnp.broadcast_to(chunk_decay, (nc, 1, RET_HEADS, 1, 1))
    prev = chunk_state_scan(jnp.moveaxis(states, 1, 0), chunk_decay)
    prev = jnp.moveaxis(prev, 0, 1)
    q_decay = jnp.exp((pos + 1.0)[:, None] * log_gamma[None, :])
    y_inter = jnp.einsum("bclhd,bchdv,lh->bclhv", qc, prev, q_decay)

    y = (y_intra + y_inter).reshape(bsz, seqlen, RET_HEADS, RET_V_DIM)
    mu = jnp.mean(y, axis=-1, keepdims=True)
    var = jnp.mean(jnp.square(y - mu), axis=-1, keepdims=True)
    y = ((y - mu) * lax.rsqrt(var + EPS)).reshape(bsz, seqlen, RET_WIDTH)
    y = y * norm_w.astype(jnp.float32) * jax.nn.silu(g.astype(jnp.float32))
    return y.astype(g.dtype)


def setup_inputs(seed: int = 0) -> dict:
    key = jax.random.key(seed)
    ks = jax.random.split(key, 17)
    f32 = jnp.float32

    def nrm(k, shape, scale):
        return jax.random.normal(k, shape, f32) * scale

    def gain(k, n):
        return 1.0 + 0.01 * jax.random.normal(k, (DEPTH, n), f32)

    dt = jnp.exp(jax.random.uniform(ks[4], (DEPTH, SSD_HEADS), f32,
                                    math.log(DT_MIN), math.log(DT_MAX)))
    dt_bias = dt + jnp.log(-jnp.expm1(-dt))
    a_log = jnp.log(jax.random.uniform(ks[5], (DEPTH, SSD_HEADS), f32, 1.0, 16.0))
    return {
        "x": jax.random.normal(ks[0], (BATCH, SEQ, D_MODEL), f32),
        "norm1_w": gain(ks[1], D_MODEL),
        "w_in": nrm(ks[2], (DEPTH, D_MODEL, D_PROJ), D_MODEL ** -0.5),
        "conv_w": nrm(ks[3], (DEPTH, CONV_WIDTH, CONV_CH), CONV_WIDTH ** -0.5),
        "conv_b": nrm(ks[6], (DEPTH, CONV_CH), 0.01),
        "dt_bias": dt_bias,
        "a_log": a_log,
        "d_skip": gain(ks[7], SSD_HEADS),
        "ssd_norm_w": gain(ks[8], SSD_WIDTH),
        "ret_norm_w": gain(ks[9], RET_WIDTH),
        "w_out": nrm(ks[10], (DEPTH, D_MIX, D_MODEL), D_MIX ** -0.5),
        "norm2_w": gain(ks[11], D_MODEL),
        "w_gate": nrm(ks[12], (DEPTH, D_MODEL, D_FF), D_MODEL ** -0.5),
        "w_up": nrm(ks[13], (DEPTH, D_MODEL, D_FF), D_MODEL ** -0.5),
        "w_down": nrm(ks[14], (DEPTH, D_FF, D_MODEL), D_FF ** -0.5),
        "final_norm_w": 1.0 + 0.01 * jax.random.normal(ks[15], (D_MODEL,), f32),
    }


def reference(x, norm1_w, w_in, conv_w, conv_b, dt_bias, a_log, d_skip, ssd_norm_w,
              ret_norm_w, w_out, norm2_w, w_gate, w_up, w_down, final_norm_w):
    for i in range(DEPTH):
        h = rmsnorm(x, norm1_w[i])
        proj = jnp.einsum("bsd,de->bse", h, w_in[i])
        z, xbc, dt_raw, q, k, v, g = jnp.split(proj, PROJ_SPLITS, axis=-1)
        y_ssd = ssd_mixer(z, xbc, dt_raw, conv_w[i], conv_b[i], dt_bias[i], a_log[i],
                          d_skip[i], ssd_norm_w[i])
        y_ret = retention_mixer(q, k, v, g, ret_norm_w[i])
        mixed = jnp.concatenate([y_ssd, y_ret], axis=-1)
        x = x + jnp.einsum("bse,ed->bsd", mixed, w_out[i])
        h = rmsnorm(x, norm2_w[i])
        a = jax.nn.silu(jnp.einsum("bsd,df->bsf", h, w_gate[i]))
        u = jnp.einsum("bsd,df->bsf", h, w_up[i])
        x = x + jnp.einsum("bsf,fd->bsd", a * u, w_down[i])
    return rmsnorm(x, final_norm_w)
```

```python
import functools
import math

import numpy as np
import jax
import jax.numpy as jnp
from jax import lax
from jax.experimental import pallas as pl
from jax.experimental.pallas import tpu as pltpu

F32 = jnp.float32
BF16 = jnp.bfloat16

D_MODEL = 1024
SSD_WIDTH = 1024
SSD_HEAD_DIM = 64
SSD_HEADS = 16
SSD_GROUPS = 2
SSD_HPG = SSD_HEADS // SSD_GROUPS
SSD_STATE = 128
GROUP_WIDTH = SSD_WIDTH // SSD_GROUPS
CONV_WIDTH = 4
BC_WIDTH = SSD_GROUPS * SSD_STATE
CONV_CH = SSD_WIDTH + 2 * BC_WIDTH
RET_WIDTH = 1024
RET_HEADS = 8
RET_V_DIM = 128
RET_QK_DIM = 64
QK_WIDTH = RET_HEADS * RET_QK_DIM
D_MIX = SSD_WIDTH + RET_WIDTH
CHUNK = 128
D_FF = 2816
ROPE_BASE = 10000.0
EPS = 1e-6
LANES = 128
SUBLANES = 8
DT_PAD = LANES

PROJ_SIZES = [SSD_WIDTH, CONV_CH, SSD_HEADS, QK_WIDTH, QK_WIDTH, RET_WIDTH, RET_WIDTH]
PROJ_OFFS = [int(v) for v in np.cumsum([0] + PROJ_SIZES)]

MIX_BLOCK = 256
FFN_BLOCK = 512
VMEM_LIMIT = 56 * 1024 * 1024

LOG_GAMMA = [math.log1p(-2.0 ** (-5.0 - h)) for h in range(RET_HEADS)]


def _dot(a, b):
    return jnp.dot(a.astype(BF16), b.astype(BF16), preferred_element_type=F32)


def _dot_nt(a, b):
    return lax.dot_general(a.astype(BF16), b.astype(BF16), (((1,), (1,)), ((), ())),
                           preferred_element_type=F32)


def _dot_tn(a, b):
    return lax.dot_general(a.astype(BF16), b.astype(BF16), (((0,), (0,)), ((), ())),
                           preferred_element_type=F32)


def _split3(x):
    hi = x.astype(BF16)
    r1 = x - hi.astype(F32)
    mid = r1.astype(BF16)
    lo = (r1 - mid.astype(F32)).astype(BF16)
    return jnp.concatenate([hi, mid, lo], axis=1)


def _silu(x):
    return x / (1.0 + jnp.exp(-x))


def _rmsnorm(x, w):
    return x * lax.rsqrt(jnp.mean(x * x, axis=-1, keepdims=True) + EPS) * w


def _mixer_kernel(x_ref, n1w_ref, wz_ref, wxbc_ref, wdt_ref, wq_ref, wk_ref, wv_ref, wg_ref,
                  convw_ref, convb_ref, dtb_ref, alog_ref, dskip_ref, ssdnw_ref, retnw_ref,
                  wout_ref, cos_ref, sin_ref,
                  out_ref,
                  z_s, xbc_s, xa_s, dt_s, q_s, k_s, v_s, g_s, mixed_s,
                  ssd_state, ret_state, tri_s, e3_s, idec_s, kd_s, qd_s):
    blk = x_ref.shape[1]
    n_chunks = blk // CHUNK
    b = pl.program_id(0)
    j = pl.program_id(1)

    @pl.when(jnp.logical_and(b == 0, j == 0))
    def _build_constants():
        row = lax.broadcasted_iota(jnp.int32, (CHUNK, CHUNK), 0)
        col = lax.broadcasted_iota(jnp.int32, (CHUNK, CHUNK), 1)
        tri_s[...] = jnp.where(col <= row, 1.0, 0.0).astype(BF16)
        er = lax.broadcasted_iota(jnp.int32, (3 * DT_PAD, SSD_WIDTH), 0) % DT_PAD
        ec = lax.broadcasted_iota(jnp.int32, (3 * DT_PAD, SSD_WIDTH), 1) // SSD_HEAD_DIM
        e3_s[...] = jnp.where(er == ec, 1.0, 0.0).astype(BF16)
        diff = (row - col).astype(F32)
        for h in range(RET_HEADS):
            idec_s[h] = jnp.where(diff >= 0.0, jnp.exp(jnp.maximum(diff, 0.0) * LOG_GAMMA[h]), 0.0)
        pos = lax.broadcasted_iota(jnp.int32, (CHUNK, RET_WIDTH), 0).astype(F32)
        hidx = lax.broadcasted_iota(jnp.int32, (CHUNK, RET_WIDTH), 1) // RET_V_DIM
        lg = jnp.zeros((CHUNK, RET_WIDTH), F32)
        for h in range(RET_HEADS):
            lg = jnp.where(hidx == h, LOG_GAMMA[h], lg)
        kd_s[...] = jnp.exp((CHUNK - 1.0 - pos) * lg)
        qd_s[...] = jnp.exp((pos + 1.0) * lg)

    @pl.when(j == 0)
    def _reset_state():
        xbc_s[0:SUBLANES, :] = jnp.zeros((SUBLANES, CONV_CH), F32)
        ssd_state[...] = jnp.zeros(ssd_state.shape, F32)
        ret_state[...] = jnp.zeros(ret_state.shape, F32)

    h = _rmsnorm(x_ref[0], n1w_ref[...]).astype(BF16)
    z_s[...] = jnp.dot(h, wz_ref[...], preferred_element_type=F32)
    xbc_s[SUBLANES:SUBLANES + blk, :] = jnp.dot(h, wxbc_ref[...], preferred_element_type=F32)
    dt_s[...] = jnp.dot(h, wdt_ref[...], preferred_element_type=F32)
    q_s[...] = jnp.dot(h, wq_ref[...], preferred_element_type=F32)
    k_s[...] = jnp.dot(h, wk_ref[...], preferred_element_type=F32)
    v_s[...] = jnp.dot(h, wv_ref[...], preferred_element_type=F32)
    g_s[...] = jnp.dot(h, wg_ref[...], preferred_element_type=F32)

    conv = convb_ref[...]
    for tap in range(CONV_WIDTH):
        start = SUBLANES - (CONV_WIDTH - 1) + tap
        conv = conv + convw_ref[tap:tap + 1, :] * xbc_s[start:start + blk, :]
    xa_s[...] = _silu(conv)
    xbc_s[0:SUBLANES, :] = xbc_s[blk:blk + SUBLANES, :]

    a_neg = -jnp.exp(alog_ref[...])
    lane = lax.broadcasted_iota(jnp.int32, (1, LANES), 1)
    head_mask = [jnp.where((lane // 32) % 2 == hh, 1.0, 0.0).astype(F32) for hh in range(2)]
    row_i = lax.broadcasted_iota(jnp.int32, (CHUNK, CHUNK), 0)
    col_i = lax.broadcasted_iota(jnp.int32, (CHUNK, CHUNK), 1)
    causal = col_i <= row_i

    def chunk_body(c, carry):
        r0 = pl.multiple_of(c * CHUNK, CHUNK)
        rows = pl.ds(r0, CHUNK)

        dt_pre = dt_s[rows, :] + dtb_ref[...]
        dt = jnp.maximum(dt_pre, 0.0) + jnp.log1p(jnp.exp(-jnp.abs(dt_pre)))
        lane_ok = lane < SSD_HEADS
        dt = jnp.where(lane_ok, dt, 0.0)
        dta = dt * a_neg
        cs3 = jnp.dot(tri_s[...], _split3(dta), preferred_element_type=F32)
        cs = cs3[:, 0:DT_PAD] + cs3[:, DT_PAD:2 * DT_PAD] + cs3[:, 2 * DT_PAD:3 * DT_PAD]
        cs_t = cs.T
        cs_e = jnp.dot(_split3(cs), e3_s[...], preferred_element_type=F32)
        dt_e = jnp.dot(_split3(dt), e3_s[...], preferred_element_type=F32)
        cs_last = cs_e[CHUNK - 1:CHUNK, :]
        exp_cs = jnp.exp(cs_e)
        dte = jnp.exp(cs_last - cs_e)
        chunk_decay = jnp.exp(cs_last)

        xs = xa_s[rows, 0:SSD_WIDTH]
        xdt = xs * dt_e
        xdt_b = xdt.astype(BF16)
        xdtd_b = (xdt * dte).astype(BF16)
        y_groups = []
        for g in range(SSD_GROUPS):
            gs = slice(g * GROUP_WIDTH, (g + 1) * GROUP_WIDTH)
            bg = xa_s[rows, SSD_WIDTH + g * SSD_STATE:SSD_WIDTH + (g + 1) * SSD_STATE].astype(BF16)
            cg = xa_s[rows, SSD_WIDTH + BC_WIDTH + g * SSD_STATE:
                      SSD_WIDTH + BC_WIDTH + (g + 1) * SSD_STATE].astype(BF16)
            cb = _dot_nt(cg, bg)
            prev = ssd_state[g]
            y_off = _dot(cg, prev) * exp_cs[:, gs]
            ssd_state[g] = prev * chunk_decay[:, gs] + _dot_tn(bg, xdtd_b[:, gs])
            y_heads = []
            for hh in range(SSD_HPG):
                hd = g * SSD_HPG + hh
                seg = cs[:, hd:hd + 1] - cs_t[hd:hd + 1, :]
                decay = jnp.exp(jnp.where(causal, seg, -jnp.inf))
                m = (cb * decay).astype(BF16)
                y_heads.append(jnp.dot(m, xdt_b[:, hd * SSD_HEAD_DIM:(hd + 1) * SSD_HEAD_DIM],
                                       preferred_element_type=F32))
            y_groups.append(jnp.concatenate(y_heads, axis=1) + y_off)
        y = jnp.concatenate(y_groups, axis=1) + xs * dskip_ref[...]
        y = y * _silu(z_s[rows, :])
        for g in range(SSD_GROUPS):
            gs = slice(g * GROUP_WIDTH, (g + 1) * GROUP_WIDTH)
            yg = y[:, gs]
            yg = yg * lax.rsqrt(jnp.mean(yg * yg, axis=-1, keepdims=True) + EPS)
            mixed_s[rows, gs] = (yg * ssdnw_ref[:, gs]).astype(BF16)

        cos = cos_ref[rows, :]
        sin = sin_ref[rows, :]
        for p in range(RET_HEADS // 2):
            ts = slice(p * LANES, (p + 1) * LANES)
            qt = q_s[rows, ts]
            kt = k_s[rows, ts]
            qr = (qt * cos + pltpu.roll(qt, 64, axis=1) * sin).astype(BF16)
            kr = (kt * cos + pltpu.roll(kt, 64, axis=1) * sin) * (RET_QK_DIM ** -0.5)
            for hh in range(2):
                hd = 2 * p + hh
                vs = slice(hd * RET_V_DIM, (hd + 1) * RET_V_DIM)
                km = (kr * head_mask[hh]).astype(BF16)
                v_h = v_s[rows, vs]
                scores = _dot_nt(qr, km) * idec_s[hd]
                prev = ret_state[hd]
                yr = _dot(scores, v_h) + _dot(qr, prev) * qd_s[:, vs]
                ret_state[hd] = prev * math.exp(CHUNK * LOG_GAMMA[hd]) + _dot_tn(km, v_h * kd_s[:, vs])
                mu = jnp.mean(yr, axis=-1, keepdims=True)
                yc = yr - mu
                var = jnp.mean(yc * yc, axis=-1, keepdims=True)
                yn = yc * lax.rsqrt(var + EPS)
                yn = yn * retnw_ref[:, vs] * _silu(g_s[rows, vs])
                mixed_s[rows, SSD_WIDTH + hd * RET_V_DIM:SSD_WIDTH + (hd + 1) * RET_V_DIM] = yn.astype(BF16)
        return carry

    lax.fori_loop(0, n_chunks, chunk_body, 0)

    out_ref[0] = x_ref[0] + jnp.dot(mixed_s[...], wout_ref[...], preferred_element_type=F32)


def _ffn_kernel(x_ref, n2w_ref, wg_ref, wu_ref, wd_ref, fnw_ref, out_ref):
    x = x_ref[...]
    h = _rmsnorm(x, n2w_ref[...]).astype(BF16)
    gate = jnp.dot(h, wg_ref[...], preferred_element_type=F32)
    up = jnp.dot(h, wu_ref[...], preferred_element_type=F32)
    act = (_silu(gate) * up).astype(BF16)
    y = x + jnp.dot(act, wd_ref[...], preferred_element_type=F32)
    out_ref[...] = _rmsnorm(y, fnw_ref[...])


def _resident(shape):
    nd = len(shape)
    return pl.BlockSpec(shape, lambda *_: (0,) * nd, pipeline_mode=pl.Buffered(1))


def _qk_perm():
    half = RET_QK_DIM // 2
    idx = []
    for p in range(RET_HEADS // 2):
        for part in range(2):
            for hh in range(2):
                base = (2 * p + hh) * RET_QK_DIM + part * half
                idx.extend(range(base, base + half))
    return np.asarray(idx, dtype=np.int32)


def _rope_tables(seqlen):
    half = RET_QK_DIM // 2
    inv_freq = ROPE_BASE ** (-jnp.arange(half, dtype=F32) / half)
    ang = jnp.arange(seqlen, dtype=F32)[:, None] * inv_freq[None, :]
    cos = jnp.cos(ang)
    sin = jnp.sin(ang)
    return (jnp.concatenate([cos, cos, cos, cos], axis=1),
            jnp.concatenate([-sin, -sin, sin, sin], axis=1))


def _layer(x, norm1_w, w_in, conv_w, conv_b, dt_bias, a_log, d_skip, ssd_norm_w,
           ret_norm_w, w_out, norm2_w, w_gate, w_up, w_down, out_norm_w):
    bsz, seqlen, _ = x.shape
    blk = MIX_BLOCK
    o = PROJ_OFFS
    perm = _qk_perm()
    wz = w_in[:, o[0]:o[1]].astype(BF16)
    wxbc = w_in[:, o[1]:o[2]].astype(BF16)
    wdt = jnp.pad(w_in[:, o[2]:o[3]], ((0, 0), (0, DT_PAD - SSD_HEADS))).astype(BF16)
    wq = w_in[:, o[3]:o[4]][:, perm].astype(BF16)
    wk = w_in[:, o[4]:o[5]][:, perm].astype(BF16)
    wv = w_in[:, o[5]:o[6]].astype(BF16)
    wg = w_in[:, o[6]:o[7]].astype(BF16)
    pad16 = lambda v: jnp.pad(v.astype(F32)[None, :], ((0, 0), (0, DT_PAD - SSD_HEADS)))
    cos_t, sin_t = _rope_tables(seqlen)
    row = lambda v: v.astype(F32)[None, :]

    mixer_in = [
        x, row(norm1_w), wz, wxbc, wdt, wq, wk, wv, wg,
        conv_w.astype(F32), row(conv_b), pad16(dt_bias), pad16(a_log),
        row(jnp.repeat(d_skip, SSD_HEAD_DIM)), row(ssd_norm_w), row(ret_norm_w),
        w_out.astype(BF16), cos_t, sin_t,
    ]
    in_specs = [pl.BlockSpec((1, blk, D_MODEL), lambda b, j: (b, j, 0))]
    in_specs += [_resident(a.shape) for a in mixer_in[1:17]]
    in_specs += [pl.BlockSpec((blk, LANES), lambda b, j: (j, 0))] * 2
    scratch = [
        pltpu.VMEM((blk, SSD_WIDTH), F32),
        pltpu.VMEM((blk + 2 * SUBLANES, CONV_CH), F32),
        pltpu.VMEM((blk, CONV_CH), F32),
        pltpu.VMEM((blk, DT_PAD), F32),
        pltpu.VMEM((blk, QK_WIDTH), F32),
        pltpu.VMEM((blk, QK_WIDTH), F32),
        pltpu.VMEM((blk, RET_WIDTH), F32),
        pltpu.VMEM((blk, RET_WIDTH), F32),
        pltpu.VMEM((blk, D_MIX), BF16),
        pltpu.VMEM((SSD_GROUPS, SSD_STATE, GROUP_WIDTH), F32),
        pltpu.VMEM((RET_HEADS, LANES, RET_V_DIM), F32),
        pltpu.VMEM((CHUNK, CHUNK), BF16),
        pltpu.VMEM((3 * DT_PAD, SSD_WIDTH), BF16),
        pltpu.VMEM((RET_HEADS, CHUNK, CHUNK), F32),
        pltpu.VMEM((CHUNK, RET_WIDTH), F32),
        pltpu.VMEM((CHUNK, RET_WIDTH), F32),
    ]
    x1 = pl.pallas_call(
        _mixer_kernel,
        grid=(bsz, seqlen // blk),
        in_specs=in_specs,
        out_specs=pl.BlockSpec((1, blk, D_MODEL), lambda b, j: (b, j, 0)),
        out_shape=jax.ShapeDtypeStruct(x.shape, F32),
        scratch_shapes=scratch,
        compiler_params=pltpu.CompilerParams(
            dimension_semantics=("arbitrary", "arbitrary"), vmem_limit_bytes=VMEM_LIMIT),
        name="mixer",
    )(*mixer_in)

    tokens = bsz * seqlen
    fblk = FFN_BLOCK
    ffn_in = [x1.reshape(tokens, D_MODEL), row(norm2_w), w_gate.astype(BF16), w_up.astype(BF16),
              w_down.astype(BF16), row(out_norm_w)]
    out = pl.pallas_call(
        _ffn_kernel,
        grid=(tokens // fblk,),
        in_specs=[pl.BlockSpec((fblk, D_MODEL), lambda i: (i, 0))] + [_resident(a.shape) for a in ffn_in[1:]],
        out_specs=pl.BlockSpec((fblk, D_MODEL), lambda i: (i, 0)),
        out_shape=jax.ShapeDtypeStruct((tokens, D_MODEL), F32),
        compiler_params=pltpu.CompilerParams(
            dimension_semantics=("arbitrary",), vmem_limit_bytes=VMEM_LIMIT),
        name="ffn",
    )(*ffn_in)
    return out.reshape(x.shape)


def kernel(x, norm1_w, w_in, conv_w, conv_b, dt_bias, a_log, d_skip, ssd_norm_w, ret_norm_w,
           w_out, norm2_w, w_gate, w_up, w_down, final_norm_w):
    depth = w_in.shape[0]
    assert depth == 1
    return _layer(x, norm1_w[0], w_in[0], conv_w[0], conv_b[0], dt_bias[0], a_log[0], d_skip[0],
                  ssd_norm_w[0], ret_norm_w[0], w_out[0], norm2_w[0], w_gate[0], w_up[0],
                  w_down[0], final_norm_w)
```

```python
import functools
import math

import numpy as np
import jax
import jax.numpy as jnp
from jax import lax
from jax.experimental import pallas as pl
from jax.experimental.pallas import tpu as pltpu

F32 = jnp.float32
BF16 = jnp.bfloat16

D_MODEL = 1024
SSD_WIDTH = 1024
SSD_HEAD_DIM = 64
SSD_HEADS = 16
SSD_GROUPS = 2
SSD_HPG = SSD_HEADS // SSD_GROUPS
SSD_STATE = 128
GROUP_WIDTH = SSD_WIDTH // SSD_GROUPS
CONV_WIDTH = 4
BC_WIDTH = SSD_GROUPS * SSD_STATE
CONV_CH = SSD_WIDTH + 2 * BC_WIDTH
RET_WIDTH = 1024
RET_HEADS = 8
RET_PAIRS = RET_HEADS // 2
RET_V_DIM = 128
RET_QK_DIM = 64
QK_WIDTH = RET_HEADS * RET_QK_DIM
D_MIX = SSD_WIDTH + RET_WIDTH
CHUNK = 128
D_FF = 2816
ROPE_BASE = 10000.0
EPS = 1e-6
LANES = 128
SUBLANES = 8
DT_PAD = LANES

PROJ_SIZES = [SSD_WIDTH, CONV_CH, SSD_HEADS, QK_WIDTH, QK_WIDTH, RET_WIDTH, RET_WIDTH]
PROJ_OFFS = [int(v) for v in np.cumsum([0] + PROJ_SIZES)]

MIX_BLOCK = 256
FFN_BLOCK = 512
PROJ_TILE = 512
CONV_TILE = 256
VMEM_LIMIT = 56 * 1024 * 1024

LOG_GAMMA = [math.log1p(-2.0 ** (-5.0 - h)) for h in range(RET_HEADS)]


def _dot(a, b):
    return jnp.dot(a.astype(BF16), b.astype(BF16), preferred_element_type=F32)


def _dot_nt(a, b):
    return lax.dot_general(a.astype(BF16), b.astype(BF16), (((1,), (1,)), ((), ())),
                           preferred_element_type=F32)


def _dot_tn(a, b):
    return lax.dot_general(a.astype(BF16), b.astype(BF16), (((0,), (0,)), ((), ())),
                           preferred_element_type=F32)


def _split(x, terms):
    parts = []
    r = x
    for i in range(terms):
        p = r.astype(BF16)
        parts.append(p)
        if i + 1 < terms:
            r = r - p.astype(F32)
    return jnp.concatenate(parts, axis=1)


def _silu(x):
    return x / (1.0 + jnp.exp(-x))


def _rmsnorm(x, w):
    return x * lax.rsqrt(jnp.mean(x * x, axis=-1, keepdims=True) + EPS) * w


class _Set:
    def __init__(self, refs):
        self.x, self.z, self.xa, self.dt, self.q, self.k, self.v, self.g = refs


def _mixer_kernel(nblk_seq, nblk_total,
                  x_ref, n1w_ref, wz_ref, wxbc_ref, wdt_ref, wq_ref, wk_ref, wv_ref, wg_ref,
                  convw_ref, convb_ref, dtb_ref, alog_ref, dskip_ref, ssdnw_ref, retnw_ref,
                  wout_ref, cos_ref, sin_ref,
                  out_ref, *scratch):
    set_a = _Set(scratch[0:8])
    set_b = _Set(scratch[8:16])
    (xbc_s, h_s, y_s, mixed_s, ssd_state, ret_state, tri_s, e2_s, qg_s, kg_s) = scratch[16:]
    blk = x_ref.shape[0]
    n_chunks = blk // CHUNK
    t = pl.program_id(0)

    @pl.when(t == 0)
    def _build_constants():
        row = lax.broadcasted_iota(jnp.int32, (CHUNK, CHUNK), 0)
        col = lax.broadcasted_iota(jnp.int32, (CHUNK, CHUNK), 1)
        tri_s[...] = jnp.where(col <= row, 1.0, 0.0).astype(BF16)
        er = lax.broadcasted_iota(jnp.int32, (2 * DT_PAD, SSD_WIDTH), 0) % DT_PAD
        ec = lax.broadcasted_iota(jnp.int32, (2 * DT_PAD, SSD_WIDTH), 1) // SSD_HEAD_DIM
        e2_s[...] = jnp.where(er == ec, 1.0, 0.0).astype(BF16)
        pos = lax.broadcasted_iota(jnp.int32, (CHUNK, LANES), 0).astype(F32)
        lane_head = (lax.broadcasted_iota(jnp.int32, (CHUNK, LANES), 1) // 32) % 2
        for p in range(RET_PAIRS):
            lg = jnp.where(lane_head == 0, LOG_GAMMA[2 * p], LOG_GAMMA[2 * p + 1])
            qg_s[p] = jnp.exp(pos * lg)
            kgam = jnp.exp(-pos * lg) * (RET_QK_DIM ** -0.5)
            for hh in range(2):
                kg_s[2 * p + hh] = jnp.where(lane_head == hh, kgam, 0.0)

    @pl.when(jnp.logical_and(t % nblk_seq == 0, t < nblk_total))
    def _reset_conv_carry():
        xbc_s[0:SUBLANES, :] = jnp.zeros((SUBLANES, CONV_CH), F32)

    @pl.when(jnp.logical_and(t > 0, (t + nblk_seq - 1) % nblk_seq == 0))
    def _reset_state():
        ssd_state[...] = jnp.zeros(ssd_state.shape, F32)
        ret_state[...] = jnp.zeros(ret_state.shape, F32)

    def project_units(dst):
        def norm():
            x = x_ref[...]
            dst.x[...] = x
            h_s[...] = _rmsnorm(x, n1w_ref[...]).astype(BF16)

        def proj(w_ref, store, c0, c1):
            store(c0, c1, jnp.dot(h_s[...], w_ref[:, c0:c1], preferred_element_type=F32))

        def store_to(ref, row0=0):
            def store(c0, c1, val):
                ref[row0:row0 + blk, c0:c1] = val
            return store

        def conv(c0, c1):
            acc = convb_ref[:, c0:c1]
            for tap in range(CONV_WIDTH):
                start = SUBLANES - (CONV_WIDTH - 1) + tap
                acc = acc + convw_ref[tap:tap + 1, c0:c1] * xbc_s[start:start + blk, c0:c1]
            dst.xa[:, c0:c1] = _silu(acc)

        def carry():
            xbc_s[0:SUBLANES, :] = xbc_s[blk:blk + SUBLANES, :]

        def tiles(width, step):
            return [(c, c + step) for c in range(0, width, step)]

        units = [norm]
        units += [functools.partial(proj, wxbc_ref, store_to(xbc_s, SUBLANES), *cc)
                  for cc in tiles(CONV_CH, PROJ_TILE)]
        for w_ref, ref in ((wz_ref, dst.z), (wdt_ref, dst.dt), (wq_ref, dst.q), (wk_ref, dst.k),
                           (wv_ref, dst.v), (wg_ref, dst.g)):
            width = ref.shape[1]
            units += [functools.partial(proj, w_ref, store_to(ref), *cc)
                      for cc in tiles(width, min(width, PROJ_TILE))]
        units += [functools.partial(conv, *cc) for cc in tiles(CONV_CH, CONV_TILE)]
        units.append(carry)
        return units

    a_neg = -jnp.exp(alog_ref[...])
    lane = lax.broadcasted_iota(jnp.int32, (1, LANES), 1)
    lane_ok = lane < SSD_HEADS
    low_half = lane < SSD_HEAD_DIM
    row_i = lax.broadcasted_iota(jnp.int32, (CHUNK, CHUNK), 0)
    col_i = lax.broadcasted_iota(jnp.int32, (CHUNK, CHUNK), 1)
    causal = col_i <= row_i

    def chunk_units(src, c):
        rows = slice(c * CHUNK, (c + 1) * CHUNK)
        env = {}

        def ssd_prep():
            dt_pre = src.dt[rows, :] + dtb_ref[...]
            dt = jnp.maximum(dt_pre, 0.0) + jnp.log1p(jnp.exp(-jnp.abs(dt_pre)))
            dt = jnp.where(lane_ok, dt, 0.0)
            dta = dt * a_neg
            cs3 = jnp.dot(tri_s[...], _split(dta, 3), preferred_element_type=F32)
            cs = cs3[:, 0:DT_PAD] + cs3[:, DT_PAD:2 * DT_PAD] + cs3[:, 2 * DT_PAD:3 * DT_PAD]
            env["cs"] = cs
            env["cs_t"] = cs.T
            cs_last = cs[CHUNK - 1:CHUNK, :]
            dt_e = jnp.dot(_split(dt, 2), e2_s[...], preferred_element_type=F32)
            ecs_e = jnp.dot(_split(jnp.exp(cs), 2), e2_s[...], preferred_element_type=F32)
            dte_e = jnp.dot(_split(jnp.exp(cs_last - cs), 2), e2_s[...], preferred_element_type=F32)
            env["ecs_e"] = ecs_e
            env["chunk_decay"] = ecs_e[CHUNK - 1:CHUNK, :]
            xdt = src.xa[rows, 0:SSD_WIDTH] * dt_e
            env["xdt"] = xdt
            env["xdtd_b"] = (xdt * dte_e).astype(BF16)

        def ssd_group(g):
            gs = slice(g * GROUP_WIDTH, (g + 1) * GROUP_WIDTH)
            bg = src.xa[rows, SSD_WIDTH + g * SSD_STATE:SSD_WIDTH + (g + 1) * SSD_STATE].astype(BF16)
            cg = src.xa[rows, SSD_WIDTH + BC_WIDTH + g * SSD_STATE:
                        SSD_WIDTH + BC_WIDTH + (g + 1) * SSD_STATE].astype(BF16)
            env["cb"] = _dot_nt(cg, bg)
            prev = ssd_state[g]
            env["y_off"] = _dot(cg, prev) * env["ecs_e"][:, gs]
            ssd_state[g] = prev * env["chunk_decay"][:, gs] + _dot_tn(bg, env["xdtd_b"][:, gs])

        def ssd_tile(g, tl):
            ts = slice(g * GROUP_WIDTH + tl * LANES, g * GROUP_WIDTH + (tl + 1) * LANES)
            cs, cs_t, cb = env["cs"], env["cs_t"], env["cb"]
            ms = []
            for hh in range(2):
                hd = g * SSD_HPG + 2 * tl + hh
                seg = cs[:, hd:hd + 1] - cs_t[hd:hd + 1, :]
                decay = jnp.exp(jnp.where(causal, seg, -jnp.inf))
                ms.append((cb * decay).astype(BF16))
            xt = env["xdt"][:, ts]
            rhs = jnp.concatenate([jnp.where(low_half, xt, 0.0).astype(BF16),
                                   jnp.where(low_half, 0.0, xt).astype(BF16)], axis=0)
            y = jnp.dot(jnp.concatenate(ms, axis=1), rhs, preferred_element_type=F32)
            y = y + env["y_off"][:, tl * LANES:(tl + 1) * LANES] + src.xa[rows, ts] * dskip_ref[:, ts]
            y_s[:, ts] = y * _silu(src.z[rows, ts])

        def ssd_norm(g):
            gs = slice(g * GROUP_WIDTH, (g + 1) * GROUP_WIDTH)
            yg = y_s[:, gs]
            yg = yg * lax.rsqrt(jnp.mean(yg * yg, axis=-1, keepdims=True) + EPS)
            mixed_s[rows, gs] = (yg * ssdnw_ref[:, gs]).astype(BF16)

        def ret_pair(p):
            ts = slice(p * LANES, (p + 1) * LANES)
            cos = cos_ref[rows, :]
            sin = sin_ref[rows, :]
            qt = src.q[rows, ts]
            kt = src.k[rows, ts]
            qg = ((qt * cos + pltpu.roll(qt, 64, axis=1) * sin) * qg_s[p]).astype(BF16)
            kr = kt * cos + pltpu.roll(kt, 64, axis=1) * sin
            kg = [(kr * kg_s[2 * p + hh]).astype(BF16) for hh in range(2)]
            scores = _dot_nt(qg, jnp.concatenate(kg, axis=0))
            for hh in range(2):
                hd = 2 * p + hh
                vs = slice(hd * RET_V_DIM, (hd + 1) * RET_V_DIM)
                v_h = src.v[rows, vs].astype(BF16)
                probs = jnp.where(causal, scores[:, hh * CHUNK:(hh + 1) * CHUNK], 0.0).astype(BF16)
                u_prev = ret_state[hd]
                yr = jnp.dot(jnp.concatenate([probs, qg], axis=1),
                             jnp.concatenate([v_h, u_prev.astype(BF16)], axis=0),
                             preferred_element_type=F32)
                ret_state[hd] = (u_prev + _dot_tn(kg[hh], v_h)) * math.exp(CHUNK * LOG_GAMMA[hd])
                mu = jnp.mean(yr, axis=-1, keepdims=True)
                yc = yr - mu
                var = jnp.mean(yc * yc, axis=-1, keepdims=True)
                yn = yc * lax.rsqrt(var + EPS)
                yn = yn * retnw_ref[:, vs] * _silu(src.g[rows, vs])
                mixed_s[rows, SSD_WIDTH + hd * RET_V_DIM:SSD_WIDTH + (hd + 1) * RET_V_DIM] = yn.astype(BF16)

        units = [ssd_prep]
        for g in range(SSD_GROUPS):
            units.append(functools.partial(ssd_group, g))
            units += [functools.partial(ssd_tile, g, tl) for tl in range(GROUP_WIDTH // LANES)]
            units.append(functools.partial(ssd_norm, g))
        units += [functools.partial(ret_pair, p) for p in range(RET_PAIRS)]
        return units

    def scan_units(src):
        def out_proj(c0, c1):
            out_ref[:, c0:c1] = src.x[:, c0:c1] + jnp.dot(mixed_s[...], wout_ref[:, c0:c1],
                                                           preferred_element_type=F32)
        units = []
        for c in range(n_chunks):
            units += chunk_units(src, c)
        units += [functools.partial(out_proj, c0, c0 + PROJ_TILE) for c0 in range(0, D_MODEL, PROJ_TILE)]
        return units

    def run_interleaved(a, b):
        na, nb = len(a), len(b)
        ia = ib = 0
        while ia < na or ib < nb:
            if ib >= nb or (ia < na and ia * nb <= ib * na):
                a[ia]()
                ia += 1
            else:
                b[ib]()
                ib += 1

    steady = jnp.logical_and(t > 0, t < nblk_total)

    @pl.when(t == 0)
    def _first():
        run_interleaved(project_units(set_a), [])

    @pl.when(jnp.logical_and(steady, t % 2 == 1))
    def _odd():
        run_interleaved(scan_units(set_a), project_units(set_b))

    @pl.when(jnp.logical_and(steady, t % 2 == 0))
    def _even():
        run_interleaved(scan_units(set_b), project_units(set_a))

    @pl.when(t == nblk_total)
    def _last():
        run_interleaved(scan_units(set_b if nblk_total % 2 == 0 else set_a), [])


def _ffn_kernel(x_ref, n2w_ref, wg_ref, wu_ref, wd_ref, fnw_ref, out_ref):
    x = x_ref[...]
    h = _rmsnorm(x, n2w_ref[...]).astype(BF16)
    gate = jnp.dot(h, wg_ref[...], preferred_element_type=F32)
    up = jnp.dot(h, wu_ref[...], preferred_element_type=F32)
    act = (_silu(gate) * up).astype(BF16)
    y = x + jnp.dot(act, wd_ref[...], preferred_element_type=F32)
    out_ref[...] = _rmsnorm(y, fnw_ref[...])


def _resident(shape):
    nd = len(shape)
    return pl.BlockSpec(shape, lambda *_: (0,) * nd, pipeline_mode=pl.Buffered(1))


def _qk_perm():
    half = RET_QK_DIM // 2
    idx = []
    for p in range(RET_PAIRS):
        for part in range(2):
            for hh in range(2):
                base = (2 * p + hh) * RET_QK_DIM + part * half
                idx.extend(range(base, base + half))
    return np.asarray(idx, dtype=np.int32)


def _rope_tables(seqlen):
    half = RET_QK_DIM // 2
    inv_freq = ROPE_BASE ** (-jnp.arange(half, dtype=F32) / half)
    ang = jnp.arange(seqlen, dtype=F32)[:, None] * inv_freq[None, :]
    cos = jnp.cos(ang)
    sin = jnp.sin(ang)
    return (jnp.concatenate([cos, cos, cos, cos], axis=1),
            jnp.concatenate([-sin, -sin, sin, sin], axis=1))


def _layer(x, norm1_w, w_in, conv_w, conv_b, dt_bias, a_log, d_skip, ssd_norm_w,
           ret_norm_w, w_out, norm2_w, w_gate, w_up, w_down, out_norm_w):
    bsz, seqlen, _ = x.shape
    tokens = bsz * seqlen
    blk = MIX_BLOCK
    nblk_seq = seqlen // blk
    nblk_total = tokens // blk
    o = PROJ_OFFS
    perm = _qk_perm()
    wz = w_in[:, o[0]:o[1]].astype(BF16)
    wxbc = w_in[:, o[1]:o[2]].astype(BF16)
    wdt = jnp.pad(w_in[:, o[2]:o[3]], ((0, 0), (0, DT_PAD - SSD_HEADS))).astype(BF16)
    wq = w_in[:, o[3]:o[4]][:, perm].astype(BF16)
    wk = w_in[:, o[4]:o[5]][:, perm].astype(BF16)
    wv = w_in[:, o[5]:o[6]].astype(BF16)
    wg = w_in[:, o[6]:o[7]].astype(BF16)
    pad16 = lambda v: jnp.pad(v.astype(F32)[None, :], ((0, 0), (0, DT_PAD - SSD_HEADS)))
    cos_t, sin_t = _rope_tables(seqlen)
    row = lambda v: v.astype(F32)[None, :]

    mixer_in = [
        x.reshape(tokens, D_MODEL), row(norm1_w), wz, wxbc, wdt, wq, wk, wv, wg,
        conv_w.astype(F32), row(conv_b), pad16(dt_bias), pad16(a_log),
        row(jnp.repeat(d_skip, SSD_HEAD_DIM)), row(ssd_norm_w), row(ret_norm_w),
        w_out.astype(BF16), cos_t, sin_t,
    ]
    last = nblk_total - 1
    in_specs = [pl.BlockSpec((blk, D_MODEL), lambda t: (jnp.minimum(t, last), 0))]
    in_specs += [_resident(a.shape) for a in mixer_in[1:17]]
    rope_spec = pl.BlockSpec((blk, LANES), lambda t: (jnp.maximum(t - 1, 0) % nblk_seq, 0))
    in_specs += [rope_spec, rope_spec]
    act_set = [
        pltpu.VMEM((blk, D_MODEL), F32),
        pltpu.VMEM((blk, SSD_WIDTH), F32),
        pltpu.VMEM((blk, CONV_CH), F32),
        pltpu.VMEM((blk, DT_PAD), F32),
        pltpu.VMEM((blk, QK_WIDTH), F32),
        pltpu.VMEM((blk, QK_WIDTH), F32),
        pltpu.VMEM((blk, RET_WIDTH), F32),
        pltpu.VMEM((blk, RET_WIDTH), F32),
    ]
    scratch = act_set + act_set + [
        pltpu.VMEM((blk + 2 * SUBLANES, CONV_CH), F32),
        pltpu.VMEM((blk, D_MODEL), BF16),
        pltpu.VMEM((CHUNK, SSD_WIDTH), F32),
        pltpu.VMEM((blk, D_MIX), BF16),
        pltpu.VMEM((SSD_GROUPS, SSD_STATE, GROUP_WIDTH), F32),
        pltpu.VMEM((RET_HEADS, LANES, RET_V_DIM), F32),
        pltpu.VMEM((CHUNK, CHUNK), BF16),
        pltpu.VMEM((2 * DT_PAD, SSD_WIDTH), BF16),
        pltpu.VMEM((RET_PAIRS, CHUNK, LANES), F32),
        pltpu.VMEM((RET_HEADS, CHUNK, LANES), F32),
    ]
    x1 = pl.pallas_call(
        functools.partial(_mixer_kernel, nblk_seq, nblk_total),
        grid=(nblk_total + 1,),
        in_specs=in_specs,
        out_specs=pl.BlockSpec((blk, D_MODEL), lambda t: (jnp.maximum(t - 1, 0), 0)),
        out_shape=jax.ShapeDtypeStruct((tokens, D_MODEL), F32),
        scratch_shapes=scratch,
        compiler_params=pltpu.CompilerParams(
            dimension_semantics=("arbitrary",), vmem_limit_bytes=VMEM_LIMIT),
        name="mixer",
    )(*mixer_in)

    fblk = FFN_BLOCK
    ffn_in = [x1, row(norm2_w), w_gate.astype(BF16), w_up.astype(BF16),
              w_down.astype(BF16), row(out_norm_w)]
    out = pl.pallas_call(
        _ffn_kernel,
        grid=(tokens // fblk,),
        in_specs=[pl.BlockSpec((fblk, D_MODEL), lambda i: (i, 0))] + [_resident(a.shape) for a in ffn_in[1:]],
        out_specs=pl.BlockSpec((fblk, D_MODEL), lambda i: (i, 0)),
        out_shape=jax.ShapeDtypeStruct((tokens, D_MODEL), F32),
        compiler_params=pltpu.CompilerParams(
            dimension_semantics=("arbitrary",), vmem_limit_bytes=VMEM_LIMIT),
        name="ffn",
    )(*ffn_in)
    return out.reshape(x.shape)


def kernel(x, norm1_w, w_in, conv_w, conv_b, dt_bias, a_log, d_skip, ssd_norm_w, ret_norm_w,
           w_out, norm2_w, w_gate, w_up, w_down, final_norm_w):
    depth = w_in.shape[0]
    assert depth == 1
    return _layer(x, norm1_w[0], w_in[0], conv_w[0], conv_b[0], dt_bias[0], a_log[0], d_skip[0],
                  ssd_norm_w[0], ret_norm_w[0], w_out[0], norm2_w[0], w_gate[0], w_up[0],
                  w_down[0], final_norm_w)
```

```python
import functools
import math

import numpy as np
import jax
import jax.numpy as jnp
from jax import lax
from jax.experimental import pallas as pl
from jax.experimental.pallas import tpu as pltpu

F32 = jnp.float32
BF16 = jnp.bfloat16

D_MODEL = 1024
SSD_WIDTH = 1024
SSD_HEAD_DIM = 64
SSD_HEADS = 16
SSD_GROUPS = 2
SSD_HPG = SSD_HEADS // SSD_GROUPS
SSD_STATE = 128
GROUP_WIDTH = SSD_WIDTH // SSD_GROUPS
CONV_WIDTH = 4
BC_WIDTH = SSD_GROUPS * SSD_STATE
CONV_CH = SSD_WIDTH + 2 * BC_WIDTH
RET_WIDTH = 1024
RET_HEADS = 8
RET_PAIRS = RET_HEADS // 2
RET_V_DIM = 128
RET_QK_DIM = 64
QK_WIDTH = RET_HEADS * RET_QK_DIM
D_MIX = SSD_WIDTH + RET_WIDTH
CHUNK = 128
D_FF = 2816
ROPE_BASE = 10000.0
EPS = 1e-6
LANES = 128
SUBLANES = 8
DT_PAD = LANES

PROJ_SIZES = [SSD_WIDTH, CONV_CH, SSD_HEADS, QK_WIDTH, QK_WIDTH, RET_WIDTH, RET_WIDTH]
PROJ_OFFS = [int(v) for v in np.cumsum([0] + PROJ_SIZES)]

MIX_BLOCK = 256
FFN_BLOCK = 512
PROJ_TILE = 512
CONV_TILE = 256
VMEM_LIMIT = 56 * 1024 * 1024

LOG_GAMMA = [math.log1p(-2.0 ** (-5.0 - h)) for h in range(RET_HEADS)]


def _dot(a, b):
    return jnp.dot(a.astype(BF16), b.astype(BF16), preferred_element_type=F32)


def _dot_nt(a, b):
    return lax.dot_general(a.astype(BF16), b.astype(BF16), (((1,), (1,)), ((), ())),
                           preferred_element_type=F32)


def _dot_tn(a, b):
    return lax.dot_general(a.astype(BF16), b.astype(BF16), (((0,), (0,)), ((), ())),
                           preferred_element_type=F32)


def _split(x, terms):
    parts = []
    r = x
    for i in range(terms):
        p = r.astype(BF16)
        parts.append(p)
        if i + 1 < terms:
            r = r - p.astype(F32)
    return jnp.concatenate(parts, axis=1)


def _silu(x):
    return x / (1.0 + jnp.exp(-x))


def _rmsnorm(x, w):
    return x * lax.rsqrt(jnp.mean(x * x, axis=-1, keepdims=True) + EPS) * w


class _Set:
    def __init__(self, refs):
        self.x, self.z, self.xa, self.dt, self.q, self.k, self.v, self.g = refs


def _mixer_kernel(nblk_seq, nblk_total,
                  x_ref, n1w_ref, wz_ref, wxbc_ref, wdt_ref, wq_ref, wk_ref, wv_ref, wg_ref,
                  convw_ref, convb_ref, dtb_ref, alog_ref, dskip_ref, ssdnw_ref, retnw_ref,
                  wout_ref, cos_ref, sin_ref,
                  out_ref, *scratch):
    set_a = _Set(scratch[0:8])
    set_b = _Set(scratch[8:16])
    (xbc_s, h_s, y_s, mixed_s, ssd_state, ret_state, tri_s, e2_s, qg_s, kg_s) = scratch[16:]
    blk = x_ref.shape[0]
    n_chunks = blk // CHUNK
    t = pl.program_id(0)

    @pl.when(t == 0)
    def _build_constants():
        row = lax.broadcasted_iota(jnp.int32, (CHUNK, CHUNK), 0)
        col = lax.broadcasted_iota(jnp.int32, (CHUNK, CHUNK), 1)
        tri_s[...] = jnp.where(col <= row, 1.0, 0.0).astype(BF16)
        er = lax.broadcasted_iota(jnp.int32, (2 * DT_PAD, SSD_WIDTH), 0) % DT_PAD
        ec = lax.broadcasted_iota(jnp.int32, (2 * DT_PAD, SSD_WIDTH), 1) // SSD_HEAD_DIM
        e2_s[...] = jnp.where(er == ec, 1.0, 0.0).astype(BF16)
        pos = lax.broadcasted_iota(jnp.int32, (CHUNK, LANES), 0).astype(F32)
        lane_head = (lax.broadcasted_iota(jnp.int32, (CHUNK, LANES), 1) // 32) % 2
        for p in range(RET_PAIRS):
            lg = jnp.where(lane_head == 0, LOG_GAMMA[2 * p], LOG_GAMMA[2 * p + 1])
            qg_s[p] = jnp.exp(pos * lg)
            kgam = jnp.exp(-pos * lg) * (RET_QK_DIM ** -0.5)
            for hh in range(2):
                kg_s[2 * p + hh] = jnp.where(lane_head == hh, kgam, 0.0)

    @pl.when(jnp.logical_and(t % nblk_seq == 0, t < nblk_total))
    def _reset_conv_carry():
        xbc_s[0:SUBLANES, :] = jnp.zeros((SUBLANES, CONV_CH), F32)

    @pl.when(t == 0)
    def _zero_first_scan_input():
        for ref in scratch[8:16]:
            ref[...] = jnp.zeros(ref.shape, ref.dtype)

    @pl.when(jnp.logical_or(t == 0, (t + nblk_seq - 1) % nblk_seq == 0))
    def _reset_state():
        ssd_state[...] = jnp.zeros(ssd_state.shape, F32)
        ret_state[...] = jnp.zeros(ret_state.shape, F32)

    def project_units(dst):
        def norm():
            x = x_ref[...]
            dst.x[...] = x
            h_s[...] = _rmsnorm(x, n1w_ref[...]).astype(BF16)

        def proj(w_ref, store, c0, c1):
            store(c0, c1, jnp.dot(h_s[...], w_ref[:, c0:c1], preferred_element_type=F32))

        def store_to(ref, row0=0):
            def store(c0, c1, val):
                ref[row0:row0 + blk, c0:c1] = val
            return store

        def conv(c0, c1):
            u = xbc_s[0:SUBLANES + blk, c0:c1]
            u1 = pltpu.roll(u, 1, axis=0)
            w = [convw_ref[tap:tap + 1, c0:c1] for tap in range(CONV_WIDTH)]
            near = convb_ref[:, c0:c1] + w[3] * u + w[2] * u1
            far = pltpu.roll(w[1] * u + w[0] * u1, 2, axis=0)
            dst.xa[:, c0:c1] = _silu((near + far)[SUBLANES:, :])

        def carry():
            xbc_s[0:SUBLANES, :] = xbc_s[blk:blk + SUBLANES, :]

        def tiles(width, step):
            return [(c, c + step) for c in range(0, width, step)]

        units = [norm]
        units += [functools.partial(proj, wxbc_ref, store_to(xbc_s, SUBLANES), *cc)
                  for cc in tiles(CONV_CH, PROJ_TILE)]
        for w_ref, ref in ((wz_ref, dst.z), (wdt_ref, dst.dt), (wq_ref, dst.q), (wk_ref, dst.k),
                           (wv_ref, dst.v), (wg_ref, dst.g)):
            width = ref.shape[1]
            units += [functools.partial(proj, w_ref, store_to(ref), *cc)
                      for cc in tiles(width, min(width, PROJ_TILE))]
        units += [functools.partial(conv, *cc) for cc in tiles(CONV_CH, CONV_TILE)]
        units.append(carry)
        return units

    a_neg = -jnp.exp(alog_ref[...])
    lane = lax.broadcasted_iota(jnp.int32, (1, LANES), 1)
    lane_ok = lane < SSD_HEADS
    low_half = lane < SSD_HEAD_DIM
    row_i = lax.broadcasted_iota(jnp.int32, (CHUNK, CHUNK), 0)
    col_i = lax.broadcasted_iota(jnp.int32, (CHUNK, CHUNK), 1)
    causal = col_i <= row_i

    def chunk_units(src, c):
        rows = slice(c * CHUNK, (c + 1) * CHUNK)
        env = {}

        def ssd_prep():
            dt_pre = src.dt[rows, :] + dtb_ref[...]
            dt = jnp.maximum(dt_pre, 0.0) + jnp.log1p(jnp.exp(-jnp.abs(dt_pre)))
            dt = jnp.where(lane_ok, dt, 0.0)
            dta = dt * a_neg
            cs3 = jnp.dot(tri_s[...], _split(dta, 3), preferred_element_type=F32)
            cs = cs3[:, 0:DT_PAD] + cs3[:, DT_PAD:2 * DT_PAD] + cs3[:, 2 * DT_PAD:3 * DT_PAD]
            env["cs"] = cs
            env["cs_t"] = cs.T
            cs_last = cs[CHUNK - 1:CHUNK, :]
            dt_e = jnp.dot(_split(dt, 2), e2_s[...], preferred_element_type=F32)
            ecs_e = jnp.dot(_split(jnp.exp(cs), 2), e2_s[...], preferred_element_type=F32)
            dte_e = jnp.dot(_split(jnp.exp(cs_last - cs), 2), e2_s[...], preferred_element_type=F32)
            env["ecs_e"] = ecs_e
            env["chunk_decay"] = ecs_e[CHUNK - 1:CHUNK, :]
            xdt = src.xa[rows, 0:SSD_WIDTH] * dt_e
            env["xdt"] = xdt
            env["xdtd_b"] = (xdt * dte_e).astype(BF16)

        def ssd_group(g):
            gs = slice(g * GROUP_WIDTH, (g + 1) * GROUP_WIDTH)
            bg = src.xa[rows, SSD_WIDTH + g * SSD_STATE:SSD_WIDTH + (g + 1) * SSD_STATE].astype(BF16)
            cg = src.xa[rows, SSD_WIDTH + BC_WIDTH + g * SSD_STATE:
                        SSD_WIDTH + BC_WIDTH + (g + 1) * SSD_STATE].astype(BF16)
            env["cb"] = _dot_nt(cg, bg)
            prev = ssd_state[g]
            env["y_off"] = _dot(cg, prev) * env["ecs_e"][:, gs]
            ssd_state[g] = prev * env["chunk_decay"][:, gs] + _dot_tn(bg, env["xdtd_b"][:, gs])

        def ssd_tile(g, tl):
            ts = slice(g * GROUP_WIDTH + tl * LANES, g * GROUP_WIDTH + (tl + 1) * LANES)
            cs, cs_t, cb = env["cs"], env["cs_t"], env["cb"]
            ms = []
            for hh in range(2):
                hd = g * SSD_HPG + 2 * tl + hh
                seg = cs[:, hd:hd + 1] - cs_t[hd:hd + 1, :]
                decay = jnp.exp(jnp.where(causal, seg, -jnp.inf))
                ms.append((cb * decay).astype(BF16))
            xt = env["xdt"][:, ts]
            rhs = jnp.concatenate([jnp.where(low_half, xt, 0.0).astype(BF16),
                                   jnp.where(low_half, 0.0, xt).astype(BF16)], axis=0)
            y = jnp.dot(jnp.concatenate(ms, axis=1), rhs, preferred_element_type=F32)
            y = y + env["y_off"][:, tl * LANES:(tl + 1) * LANES] + src.xa[rows, ts] * dskip_ref[:, ts]
            y_s[:, ts] = y * _silu(src.z[rows, ts])

        def ssd_norm(g):
            gs = slice(g * GROUP_WIDTH, (g + 1) * GROUP_WIDTH)
            yg = y_s[:, gs]
            yg = yg * lax.rsqrt(jnp.mean(yg * yg, axis=-1, keepdims=True) + EPS)
            mixed_s[rows, gs] = (yg * ssdnw_ref[:, gs]).astype(BF16)

        def ret_pair(p):
            ts = slice(p * LANES, (p + 1) * LANES)
            cos = cos_ref[rows, :]
            sin = sin_ref[rows, :]
            qt = src.q[rows, ts]
            kt = src.k[rows, ts]
            qg = ((qt * cos + pltpu.roll(qt, 64, axis=1) * sin) * qg_s[p]).astype(BF16)
            kr = kt * cos + pltpu.roll(kt, 64, axis=1) * sin
            kg = [(kr * kg_s[2 * p + hh]).astype(BF16) for hh in range(2)]
            scores = _dot_nt(qg, jnp.concatenate(kg, axis=0))
            for hh in range(2):
                hd = 2 * p + hh
                vs = slice(hd * RET_V_DIM, (hd + 1) * RET_V_DIM)
                v_h = src.v[rows, vs].astype(BF16)
                probs = jnp.where(causal, scores[:, hh * CHUNK:(hh + 1) * CHUNK], 0.0).astype(BF16)
                u_prev = ret_state[hd]
                yr = jnp.dot(jnp.concatenate([probs, qg], axis=1),
                             jnp.concatenate([v_h, u_prev.astype(BF16)], axis=0),
                             preferred_element_type=F32)
                ret_state[hd] = (u_prev + _dot_tn(kg[hh], v_h)) * math.exp(CHUNK * LOG_GAMMA[hd])
                mu = jnp.mean(yr, axis=-1, keepdims=True)
                yc = yr - mu
                var = jnp.mean(yc * yc, axis=-1, keepdims=True)
                yn = yc * lax.rsqrt(var + EPS)
                yn = yn * retnw_ref[:, vs] * _silu(src.g[rows, vs])
                mixed_s[rows, SSD_WIDTH + hd * RET_V_DIM:SSD_WIDTH + (hd + 1) * RET_V_DIM] = yn.astype(BF16)

        units = [ssd_prep]
        for g in range(SSD_GROUPS):
            units.append(functools.partial(ssd_group, g))
            units += [functools.partial(ssd_tile, g, tl) for tl in range(GROUP_WIDTH // LANES)]
            units.append(functools.partial(ssd_norm, g))
        units += [functools.partial(ret_pair, p) for p in range(RET_PAIRS)]
        return units

    def scan_units(src):
        def out_proj(c0, c1):
            out_ref[:, c0:c1] = src.x[:, c0:c1] + jnp.dot(mixed_s[...], wout_ref[:, c0:c1],
                                                           preferred_element_type=F32)
        units = []
        for c in range(n_chunks):
            units += chunk_units(src, c)
        units += [functools.partial(out_proj, c0, c0 + PROJ_TILE) for c0 in range(0, D_MODEL, PROJ_TILE)]
        return units

    def run_interleaved(a, b):
        na, nb = len(a), len(b)
        ia = ib = 0
        while ia < na or ib < nb:
            if ib >= nb or (ia < na and ia * nb <= ib * na):
                a[ia]()
                ia += 1
            else:
                b[ib]()
                ib += 1

    @pl.when(t % 2 == 1)
    def _odd():
        run_interleaved(scan_units(set_a), project_units(set_b))

    @pl.when(t % 2 == 0)
    def _even():
        run_interleaved(scan_units(set_b), project_units(set_a))


def _ffn_kernel(x_ref, n2w_ref, wg_ref, wu_ref, wd_ref, fnw_ref, out_ref):
    x = x_ref[...]
    h = _rmsnorm(x, n2w_ref[...]).astype(BF16)
    gate = jnp.dot(h, wg_ref[...], preferred_element_type=F32)
    up = jnp.dot(h, wu_ref[...], preferred_element_type=F32)
    act = (_silu(gate) * up).astype(BF16)
    y = x + jnp.dot(act, wd_ref[...], preferred_element_type=F32)
    out_ref[...] = _rmsnorm(y, fnw_ref[...])


def _resident(shape):
    nd = len(shape)
    return pl.BlockSpec(shape, lambda *_: (0,) * nd, pipeline_mode=pl.Buffered(1))


def _qk_perm():
    half = RET_QK_DIM // 2
    idx = []
    for p in range(RET_PAIRS):
        for part in range(2):
            for hh in range(2):
                base = (2 * p + hh) * RET_QK_DIM + part * half
                idx.extend(range(base, base + half))
    return np.asarray(idx, dtype=np.int32)


def _rope_tables(seqlen):
    half = RET_QK_DIM // 2
    inv_freq = ROPE_BASE ** (-jnp.arange(half, dtype=F32) / half)
    ang = jnp.arange(seqlen, dtype=F32)[:, None] * inv_freq[None, :]
    cos = jnp.cos(ang)
    sin = jnp.sin(ang)
    return (jnp.concatenate([cos, cos, cos, cos], axis=1),
            jnp.concatenate([-sin, -sin, sin, sin], axis=1))


def _layer(x, norm1_w, w_in, conv_w, conv_b, dt_bias, a_log, d_skip, ssd_norm_w,
           ret_norm_w, w_out, norm2_w, w_gate, w_up, w_down, out_norm_w):
    bsz, seqlen, _ = x.shape
    tokens = bsz * seqlen
    blk = MIX_BLOCK
    nblk_seq = seqlen // blk
    nblk_total = tokens // blk
    assert nblk_total % 2 == 0
    o = PROJ_OFFS
    perm = _qk_perm()
    wz = w_in[:, o[0]:o[1]].astype(BF16)
    wxbc = w_in[:, o[1]:o[2]].astype(BF16)
    wdt = jnp.pad(w_in[:, o[2]:o[3]], ((0, 0), (0, DT_PAD - SSD_HEADS))).astype(BF16)
    wq = w_in[:, o[3]:o[4]][:, perm].astype(BF16)
    wk = w_in[:, o[4]:o[5]][:, perm].astype(BF16)
    wv = w_in[:, o[5]:o[6]].astype(BF16)
    wg = w_in[:, o[6]:o[7]].astype(BF16)
    pad16 = lambda v: jnp.pad(v.astype(F32)[None, :], ((0, 0), (0, DT_PAD - SSD_HEADS)))
    cos_t, sin_t = _rope_tables(seqlen)
    row = lambda v: v.astype(F32)[None, :]

    mixer_in = [
        x.reshape(tokens, D_MODEL), row(norm1_w), wz, wxbc, wdt, wq, wk, wv, wg,
        conv_w.astype(F32), row(conv_b), pad16(dt_bias), pad16(a_log),
        row(jnp.repeat(d_skip, SSD_HEAD_DIM)), row(ssd_norm_w), row(ret_norm_w),
        w_out.astype(BF16), cos_t, sin_t,
    ]
    last = nblk_total - 1
    in_specs = [pl.BlockSpec((blk, D_MODEL), lambda t: (jnp.minimum(t, last), 0))]
    in_specs += [_resident(a.shape) for a in mixer_in[1:17]]
    rope_spec = pl.BlockSpec((blk, LANES), lambda t: (jnp.maximum(t - 1, 0) % nblk_seq, 0))
    in_specs += [rope_spec, rope_spec]
    act_set = [
        pltpu.VMEM((blk, D_MODEL), F32),
        pltpu.VMEM((blk, SSD_WIDTH), F32),
        pltpu.VMEM((blk, CONV_CH), F32),
        pltpu.VMEM((blk, DT_PAD), F32),
        pltpu.VMEM((blk, QK_WIDTH), F32),
        pltpu.VMEM((blk, QK_WIDTH), F32),
        pltpu.VMEM((blk, RET_WIDTH), F32),
        pltpu.VMEM((blk, RET_WIDTH), F32),
    ]
    scratch = act_set + act_set + [
        pltpu.VMEM((blk + 2 * SUBLANES, CONV_CH), F32),
        pltpu.VMEM((blk, D_MODEL), BF16),
        pltpu.VMEM((CHUNK, SSD_WIDTH), F32),
        pltpu.VMEM((blk, D_MIX), BF16),
        pltpu.VMEM((SSD_GROUPS, SSD_STATE, GROUP_WIDTH), F32),
        pltpu.VMEM((RET_HEADS, LANES, RET_V_DIM), F32),
        pltpu.VMEM((CHUNK, CHUNK), BF16),
        pltpu.VMEM((2 * DT_PAD, SSD_WIDTH), BF16),
        pltpu.VMEM((RET_PAIRS, CHUNK, LANES), F32),
        pltpu.VMEM((RET_HEADS, CHUNK, LANES), F32),
    ]
    x1 = pl.pallas_call(
        functools.partial(_mixer_kernel, nblk_seq, nblk_total),
        grid=(nblk_total + 1,),
        in_specs=in_specs,
        out_specs=pl.BlockSpec((blk, D_MODEL), lambda t: (jnp.maximum(t - 1, 0), 0)),
        out_shape=jax.ShapeDtypeStruct((tokens, D_MODEL), F32),
        scratch_shapes=scratch,
        compiler_params=pltpu.CompilerParams(
            dimension_semantics=("arbitrary",), vmem_limit_bytes=VMEM_LIMIT),
        name="mixer",
    )(*mixer_in)

    fblk = FFN_BLOCK
    ffn_in = [x1, row(norm2_w), w_gate.astype(BF16), w_up.astype(BF16),
              w_down.astype(BF16), row(out_norm_w)]
    out = pl.pallas_call(
        _ffn_kernel,
        grid=(tokens // fblk,),
        in_specs=[pl.BlockSpec((fblk, D_MODEL), lambda i: (i, 0))] + [_resident(a.shape) for a in ffn_in[1:]],
        out_specs=pl.BlockSpec((fblk, D_MODEL), lambda i: (i, 0)),
        out_shape=jax.ShapeDtypeStruct((tokens, D_MODEL), F32),
        compiler_params=pltpu.CompilerParams(
            dimension_semantics=("arbitrary",), vmem_limit_bytes=VMEM_LIMIT),
        name="ffn",
    )(*ffn_in)
    return out.reshape(x.shape)


def kernel(x, norm1_w, w_in, conv_w, conv_b, dt_bias, a_log, d_skip, ssd_norm_w, ret_norm_w,
           w_out, norm2_w, w_gate, w_up, w_down, final_norm_w):
    depth = w_in.shape[0]
    assert depth == 1
    return _layer(x, norm1_w[0], w_in[0], conv_w[0], conv_b[0], dt_bias[0], a_log[0], d_skip[0],
                  ssd_norm_w[0], ret_norm_w[0], w_out[0], norm2_w[0], w_gate[0], w_up[0],
                  w_down[0], final_norm_w)
```

```python
import functools
import math

import numpy as np
import jax
import jax.numpy as jnp
from jax import lax
from jax.experimental import pallas as pl
from jax.experimental.pallas import tpu as pltpu

F32 = jnp.float32
BF16 = jnp.bfloat16

D_MODEL = 1024
SSD_WIDTH = 1024
SSD_HEAD_DIM = 64
SSD_HEADS = 16
SSD_GROUPS = 2
SSD_HPG = SSD_HEADS // SSD_GROUPS
SSD_STATE = 128
GROUP_WIDTH = SSD_WIDTH // SSD_GROUPS
CONV_WIDTH = 4
BC_WIDTH = SSD_GROUPS * SSD_STATE
CONV_CH = SSD_WIDTH + 2 * BC_WIDTH
RET_WIDTH = 1024
RET_HEADS = 8
RET_PAIRS = RET_HEADS // 2
RET_V_DIM = 128
RET_QK_DIM = 64
QK_WIDTH = RET_HEADS * RET_QK_DIM
D_MIX = SSD_WIDTH + RET_WIDTH
CHUNK = 128
D_FF = 2816
ROPE_BASE = 10000.0
EPS = 1e-6
LANES = 128
SUBLANES = 8
DT_PAD = LANES

PROJ_SIZES = [SSD_WIDTH, CONV_CH, SSD_HEADS, QK_WIDTH, QK_WIDTH, RET_WIDTH, RET_WIDTH]
PROJ_OFFS = [int(v) for v in np.cumsum([0] + PROJ_SIZES)]

MIX_BLOCK = 256
SET_REFS = 8
FFN_BLOCK = 512
PROJ_TILE = 256
CONV_TILE = 128
VMEM_LIMIT = 56 * 1024 * 1024

LOG_GAMMA = [math.log1p(-2.0 ** (-5.0 - h)) for h in range(RET_HEADS)]


def _dot(a, b):
    return jnp.dot(a.astype(BF16), b.astype(BF16), preferred_element_type=F32)


def _dot_nt(a, b):
    return lax.dot_general(a.astype(BF16), b.astype(BF16), (((1,), (1,)), ((), ())),
                           preferred_element_type=F32)


def _dot_tn(a, b):
    return lax.dot_general(a.astype(BF16), b.astype(BF16), (((0,), (0,)), ((), ())),
                           preferred_element_type=F32)


def _split(x, terms):
    parts = []
    r = x
    for i in range(terms):
        p = r.astype(BF16)
        parts.append(p)
        if i + 1 < terms:
            r = r - p.astype(F32)
    return jnp.concatenate(parts, axis=1)


def _silu(x):
    return x / (1.0 + jnp.exp(-x))


def _rmsnorm(x, w):
    return x * lax.rsqrt(jnp.mean(x * x, axis=-1, keepdims=True) + EPS) * w


class _Set:
    def __init__(self, refs):
        self.x, self.z, self.xa, self.dt, self.q, self.k, self.v, self.g = refs


def _mixer_kernel(nblk_seq, nblk_total,
                  x_ref, n1w_ref, wz_ref, wxbc_ref, wdt_ref, wq_ref, wk_ref, wv_ref, wg_ref,
                  convw_ref, convb_ref, dtb_ref, alog_ref, dskip_ref, ssdnw_ref, retnw_ref,
                  wout_ref, cos_ref, sin_ref,
                  out_ref, *scratch):
    set_a = _Set(scratch[0:SET_REFS])
    set_b = _Set(scratch[SET_REFS:2 * SET_REFS])
    (xbc_s, h_s, y_s, mixed_s, ssd_state, ret_state, tri_s, e2_s, qg_s, kg_s) = scratch[2 * SET_REFS:]
    blk = x_ref.shape[0]
    n_chunks = blk // CHUNK
    t = pl.program_id(0)

    @pl.when(t == 0)
    def _build_constants():
        row = lax.broadcasted_iota(jnp.int32, (CHUNK, CHUNK), 0)
        col = lax.broadcasted_iota(jnp.int32, (CHUNK, CHUNK), 1)
        tri_s[...] = jnp.where(col <= row, 1.0, 0.0).astype(BF16)
        er = lax.broadcasted_iota(jnp.int32, (2 * DT_PAD, SSD_WIDTH), 0) % DT_PAD
        ec = lax.broadcasted_iota(jnp.int32, (2 * DT_PAD, SSD_WIDTH), 1) // SSD_HEAD_DIM
        e2_s[...] = jnp.where(er == ec, 1.0, 0.0).astype(BF16)
        pos = lax.broadcasted_iota(jnp.int32, (CHUNK, LANES), 0).astype(F32)
        lane_head = (lax.broadcasted_iota(jnp.int32, (CHUNK, LANES), 1) // 32) % 2
        for p in range(RET_PAIRS):
            lg = jnp.where(lane_head == 0, LOG_GAMMA[2 * p], LOG_GAMMA[2 * p + 1])
            qg_s[p] = jnp.exp(pos * lg)
            kgam = jnp.exp(-pos * lg) * (RET_QK_DIM ** -0.5)
            for hh in range(2):
                kg_s[2 * p + hh] = jnp.where(lane_head == hh, kgam, 0.0)

    @pl.when(jnp.logical_and(t % nblk_seq == 0, t < nblk_total))
    def _reset_conv_carry():
        xbc_s[0:SUBLANES, :] = jnp.zeros((SUBLANES, CONV_CH), F32)

    @pl.when(t == 0)
    def _zero_first_scan_input():
        for ref in scratch[SET_REFS:2 * SET_REFS]:
            ref[...] = jnp.zeros(ref.shape, ref.dtype)

    @pl.when(jnp.logical_or(t == 0, (t + nblk_seq - 1) % nblk_seq == 0))
    def _reset_state():
        ssd_state[...] = jnp.zeros(ssd_state.shape, F32)
        ret_state[...] = jnp.zeros(ret_state.shape, F32)

    def project_units(dst):
        def norm():
            x = x_ref[...]
            dst.x[...] = x
            h_s[...] = _rmsnorm(x, n1w_ref[...]).astype(BF16)

        def proj(w_ref, store, c0, c1):
            store(c0, c1, jnp.dot(h_s[...], w_ref[:, c0:c1], preferred_element_type=F32))

        def store_to(ref, row0=0):
            def store(c0, c1, val):
                ref[row0:row0 + blk, c0:c1] = val
            return store

        def conv(c0, c1):
            u = xbc_s[0:SUBLANES + blk, c0:c1]
            u1 = pltpu.roll(u, 1, axis=0)
            w = [convw_ref[tap:tap + 1, c0:c1] for tap in range(CONV_WIDTH)]
            near = convb_ref[:, c0:c1] + w[3] * u + w[2] * u1
            far = pltpu.roll(w[1] * u + w[0] * u1, 2, axis=0)
            dst.xa[:, c0:c1] = _silu((near + far)[SUBLANES:, :])

        def carry():
            xbc_s[0:SUBLANES, :] = xbc_s[blk:blk + SUBLANES, :]

        def tiles(width, step):
            return [(c, c + step) for c in range(0, width, step)]

        conv_in_dots = [functools.partial(proj, wxbc_ref, store_to(xbc_s, SUBLANES), *cc)
                        for cc in tiles(CONV_CH, PROJ_TILE)]
        other_dots = []
        for w_ref, ref in ((wz_ref, dst.z), (wdt_ref, dst.dt), (wq_ref, dst.q), (wk_ref, dst.k),
                           (wv_ref, dst.v), (wg_ref, dst.g)):
            width = ref.shape[1]
            other_dots += [functools.partial(proj, w_ref, store_to(ref), *cc)
                           for cc in tiles(width, min(width, PROJ_TILE))]
        convs = [functools.partial(conv, *cc) for cc in tiles(CONV_CH, CONV_TILE)]
        return norm, conv_in_dots, other_dots, convs + [carry]

    a_neg = -jnp.exp(alog_ref[...])
    lane = lax.broadcasted_iota(jnp.int32, (1, LANES), 1)
    lane_ok = lane < SSD_HEADS
    low_half = lane < SSD_HEAD_DIM
    row_i = lax.broadcasted_iota(jnp.int32, (CHUNK, CHUNK), 0)
    col_i = lax.broadcasted_iota(jnp.int32, (CHUNK, CHUNK), 1)
    causal = col_i <= row_i

    def chunk_units(src, c):
        rows = slice(c * CHUNK, (c + 1) * CHUNK)
        env = {}

        def ssd_prep():
            dt_pre = src.dt[rows, :] + dtb_ref[...]
            dt = jnp.maximum(dt_pre, 0.0) + jnp.log1p(jnp.exp(-jnp.abs(dt_pre)))
            dt = jnp.where(lane_ok, dt, 0.0)
            dta = dt * a_neg
            cs3 = jnp.dot(tri_s[...], _split(dta, 3), preferred_element_type=F32)
            cs = cs3[:, 0:DT_PAD] + cs3[:, DT_PAD:2 * DT_PAD] + cs3[:, 2 * DT_PAD:3 * DT_PAD]
            env["cs"] = cs
            env["cs_t"] = cs.T
            cs_last = cs[CHUNK - 1:CHUNK, :]
            dt_e = jnp.dot(_split(dt, 2), e2_s[...], preferred_element_type=F32)
            ecs_e = jnp.dot(_split(jnp.exp(cs), 2), e2_s[...], preferred_element_type=F32)
            dte_e = jnp.dot(_split(jnp.exp(cs_last - cs), 2), e2_s[...], preferred_element_type=F32)
            env["ecs_e"] = ecs_e
            env["chunk_decay"] = ecs_e[CHUNK - 1:CHUNK, :]
            xdt = src.xa[rows, 0:SSD_WIDTH] * dt_e
            env["xdt"] = xdt
            env["xdtd_b"] = (xdt * dte_e).astype(BF16)

        def ssd_group(g):
            gs = slice(g * GROUP_WIDTH, (g + 1) * GROUP_WIDTH)
            bg = src.xa[rows, SSD_WIDTH + g * SSD_STATE:SSD_WIDTH + (g + 1) * SSD_STATE].astype(BF16)
            cg = src.xa[rows, SSD_WIDTH + BC_WIDTH + g * SSD_STATE:
                        SSD_WIDTH + BC_WIDTH + (g + 1) * SSD_STATE].astype(BF16)
            env["cb"] = _dot_nt(cg, bg)
            prev = ssd_state[g]
            env["y_off"] = _dot(cg, prev) * env["ecs_e"][:, gs]
            ssd_state[g] = prev * env["chunk_decay"][:, gs] + _dot_tn(bg, env["xdtd_b"][:, gs])

        def ssd_tile(g, tl):
            ts = slice(g * GROUP_WIDTH + tl * LANES, g * GROUP_WIDTH + (tl + 1) * LANES)
            cs, cs_t, cb = env["cs"], env["cs_t"], env["cb"]
            ms = []
            for hh in range(2):
                hd = g * SSD_HPG + 2 * tl + hh
                seg = cs[:, hd:hd + 1] - cs_t[hd:hd + 1, :]
                decay = jnp.exp(jnp.where(causal, seg, -jnp.inf))
                ms.append((cb * decay).astype(BF16))
            xt = env["xdt"][:, ts]
            rhs = jnp.concatenate([jnp.where(low_half, xt, 0.0).astype(BF16),
                                   jnp.where(low_half, 0.0, xt).astype(BF16)], axis=0)
            y = jnp.dot(jnp.concatenate(ms, axis=1), rhs, preferred_element_type=F32)
            y = y + env["y_off"][:, tl * LANES:(tl + 1) * LANES] + src.xa[rows, ts] * dskip_ref[:, ts]
            y_s[:, ts] = y * _silu(src.z[rows, ts])

        def ssd_norm(g):
            gs = slice(g * GROUP_WIDTH, (g + 1) * GROUP_WIDTH)
            yg = y_s[:, gs]
            yg = yg * lax.rsqrt(jnp.mean(yg * yg, axis=-1, keepdims=True) + EPS)
            mixed_s[rows, gs] = (yg * ssdnw_ref[:, gs]).astype(BF16)

        def ret_pair(p):
            ts = slice(p * LANES, (p + 1) * LANES)
            cos = cos_ref[rows, :]
            sin = sin_ref[rows, :]
            qt = src.q[rows, ts]
            kt = src.k[rows, ts]
            qg = ((qt * cos + pltpu.roll(qt, 64, axis=1) * sin) * qg_s[p]).astype(BF16)
            kr = kt * cos + pltpu.roll(kt, 64, axis=1) * sin
            kg = [(kr * kg_s[2 * p + hh]).astype(BF16) for hh in range(2)]
            scores = _dot_nt(qg, jnp.concatenate(kg, axis=0))
            for hh in range(2):
                hd = 2 * p + hh
                vs = slice(hd * RET_V_DIM, (hd + 1) * RET_V_DIM)
                v_h = src.v[rows, vs].astype(BF16)
                probs = jnp.where(causal, scores[:, hh * CHUNK:(hh + 1) * CHUNK], 0.0).astype(BF16)
                u_prev = ret_state[hd]
                yr = jnp.dot(jnp.concatenate([probs, qg], axis=1),
                             jnp.concatenate([v_h, u_prev.astype(BF16)], axis=0),
                             preferred_element_type=F32)
                ret_state[hd] = (u_prev + _dot_tn(kg[hh], v_h)) * math.exp(CHUNK * LOG_GAMMA[hd])
                mu = jnp.mean(yr, axis=-1, keepdims=True)
                yc = yr - mu
                var = jnp.mean(yc * yc, axis=-1, keepdims=True)
                yn = yc * lax.rsqrt(var + EPS)
                yn = yn * retnw_ref[:, vs] * _silu(src.g[rows, vs])
                mixed_s[rows, SSD_WIDTH + hd * RET_V_DIM:SSD_WIDTH + (hd + 1) * RET_V_DIM] = yn.astype(BF16)

        units = [ssd_prep]
        for g in range(SSD_GROUPS):
            units.append(functools.partial(ssd_group, g))
            units += [functools.partial(ssd_tile, g, tl) for tl in range(GROUP_WIDTH // LANES)]
            units.append(functools.partial(ssd_norm, g))
        units += [functools.partial(ret_pair, p) for p in range(RET_PAIRS)]
        return units

    def scan_units(src):
        units = []
        for c in range(n_chunks):
            units += chunk_units(src, c)
        return units

    def out_proj_units(src):
        def out_proj(c0, c1):
            out_ref[:, c0:c1] = src.x[:, c0:c1] + jnp.dot(mixed_s[...], wout_ref[:, c0:c1],
                                                           preferred_element_type=F32)
        return [functools.partial(out_proj, c0, c0 + PROJ_TILE) for c0 in range(0, D_MODEL, PROJ_TILE)]

    def interleave(a, b):
        na, nb = len(a), len(b)
        merged = []
        ia = ib = 0
        while ia < na or ib < nb:
            if ib >= nb or (ia < na and ia * nb <= ib * na):
                merged.append(a[ia])
                ia += 1
            else:
                merged.append(b[ib])
                ib += 1
        return merged

    def step(src, dst):
        norm, conv_in_dots, other_dots, convs = project_units(dst)
        norm()
        for unit in (interleave(scan_units(src), conv_in_dots + other_dots)
                     + interleave(convs, out_proj_units(src))):
            unit()

    @pl.when(t % 2 == 1)
    def _odd():
        step(set_a, set_b)

    @pl.when(t % 2 == 0)
    def _even():
        step(set_b, set_a)


def _ffn_kernel(x_ref, n2w_ref, wg_ref, wu_ref, wd_ref, fnw_ref, out_ref):
    x = x_ref[...]
    h = _rmsnorm(x, n2w_ref[...]).astype(BF16)
    gate = jnp.dot(h, wg_ref[...], preferred_element_type=F32)
    up = jnp.dot(h, wu_ref[...], preferred_element_type=F32)
    act = (_silu(gate) * up).astype(BF16)
    y = x + jnp.dot(act, wd_ref[...], preferred_element_type=F32)
    out_ref[...] = _rmsnorm(y, fnw_ref[...])


def _resident(shape):
    nd = len(shape)
    return pl.BlockSpec(shape, lambda *_: (0,) * nd, pipeline_mode=pl.Buffered(1))


def _qk_perm():
    half = RET_QK_DIM // 2
    idx = []
    for p in range(RET_PAIRS):
        for part in range(2):
            for hh in range(2):
                base = (2 * p + hh) * RET_QK_DIM + part * half
                idx.extend(range(base, base + half))
    return np.asarray(idx, dtype=np.int32)


def _rope_tables(seqlen):
    half = RET_QK_DIM // 2
    inv_freq = ROPE_BASE ** (-jnp.arange(half, dtype=F32) / half)
    ang = jnp.arange(seqlen, dtype=F32)[:, None] * inv_freq[None, :]
    cos = jnp.cos(ang)
    sin = jnp.sin(ang)
    return (jnp.concatenate([cos, cos, cos, cos], axis=1),
            jnp.concatenate([-sin, -sin, sin, sin], axis=1))


def _layer(x, norm1_w, w_in, conv_w, conv_b, dt_bias, a_log, d_skip, ssd_norm_w,
           ret_norm_w, w_out, norm2_w, w_gate, w_up, w_down, out_norm_w):
    bsz, seqlen, _ = x.shape
    tokens = bsz * seqlen
    blk = MIX_BLOCK
    nblk_seq = seqlen // blk
    nblk_total = tokens // blk
    o = PROJ_OFFS
    perm = _qk_perm()
    wz = w_in[:, o[0]:o[1]].astype(BF16)
    wxbc = w_in[:, o[1]:o[2]].astype(BF16)
    wdt = jnp.pad(w_in[:, o[2]:o[3]], ((0, 0), (0, DT_PAD - SSD_HEADS))).astype(BF16)
    wq = w_in[:, o[3]:o[4]][:, perm].astype(BF16)
    wk = w_in[:, o[4]:o[5]][:, perm].astype(BF16)
    wv = w_in[:, o[5]:o[6]].astype(BF16)
    wg = w_in[:, o[6]:o[7]].astype(BF16)
    pad16 = lambda v: jnp.pad(v.astype(F32)[None, :], ((0, 0), (0, DT_PAD - SSD_HEADS)))
    cos_t, sin_t = _rope_tables(seqlen)
    row = lambda v: v.astype(F32)[None, :]

    mixer_in = [
        x.reshape(tokens, D_MODEL), row(norm1_w), wz, wxbc, wdt, wq, wk, wv, wg,
        conv_w.astype(F32), row(conv_b), pad16(dt_bias), pad16(a_log),
        row(jnp.repeat(d_skip, SSD_HEAD_DIM)), row(ssd_norm_w), row(ret_norm_w),
        w_out.astype(BF16), cos_t, sin_t,
    ]
    last = nblk_total - 1
    in_specs = [pl.BlockSpec((blk, D_MODEL), lambda t: (jnp.minimum(t, last), 0))]
    in_specs += [_resident(a.shape) for a in mixer_in[1:17]]
    rope_spec = pl.BlockSpec((blk, LANES), lambda t: (jnp.clip(t - 1, 0, last) % nblk_seq, 0))
    in_specs += [rope_spec, rope_spec]
    act_set = [
        pltpu.VMEM((blk, D_MODEL), F32),
        pltpu.VMEM((blk, SSD_WIDTH), F32),
        pltpu.VMEM((blk, CONV_CH), F32),
        pltpu.VMEM((blk, DT_PAD), F32),
        pltpu.VMEM((blk, QK_WIDTH), F32),
        pltpu.VMEM((blk, QK_WIDTH), F32),
        pltpu.VMEM((blk, RET_WIDTH), F32),
        pltpu.VMEM((blk, RET_WIDTH), F32),
    ]
    assert len(act_set) == SET_REFS
    scratch = act_set + act_set + [
        pltpu.VMEM((blk + 2 * SUBLANES, CONV_CH), F32),
        pltpu.VMEM((blk, D_MODEL), BF16),
        pltpu.VMEM((CHUNK, SSD_WIDTH), F32),
        pltpu.VMEM((blk, D_MIX), BF16),
        pltpu.VMEM((SSD_GROUPS, SSD_STATE, GROUP_WIDTH), F32),
        pltpu.VMEM((RET_HEADS, LANES, RET_V_DIM), F32),
        pltpu.VMEM((CHUNK, CHUNK), BF16),
        pltpu.VMEM((2 * DT_PAD, SSD_WIDTH), BF16),
        pltpu.VMEM((RET_PAIRS, CHUNK, LANES), F32),
        pltpu.VMEM((RET_HEADS, CHUNK, LANES), F32),
    ]
    x1 = pl.pallas_call(
        functools.partial(_mixer_kernel, nblk_seq, nblk_total),
        grid=(nblk_total + 1,),
        in_specs=in_specs,
        out_specs=pl.BlockSpec((blk, D_MODEL), lambda t: (jnp.maximum(t - 1, 0), 0)),
        out_shape=jax.ShapeDtypeStruct((tokens, D_MODEL), F32),
        scratch_shapes=scratch,
        compiler_params=pltpu.CompilerParams(
            dimension_semantics=("arbitrary",), vmem_limit_bytes=VMEM_LIMIT),
        name="mixer",
    )(*mixer_in)

    fblk = FFN_BLOCK
    ffn_in = [x1, row(norm2_w), w_gate.astype(BF16), w_up.astype(BF16),
              w_down.astype(BF16), row(out_norm_w)]
    out = pl.pallas_call(
        _ffn_kernel,
        grid=(tokens // fblk,),
        in_specs=[pl.BlockSpec((fblk, D_MODEL), lambda i: (i, 0))] + [_resident(a.shape) for a in ffn_in[1:]],
        out_specs=pl.BlockSpec((fblk, D_MODEL), lambda i: (i, 0)),
        out_shape=jax.ShapeDtypeStruct((tokens, D_MODEL), F32),
        compiler_params=pltpu.CompilerParams(
            dimension_semantics=("arbitrary",), vmem_limit_bytes=VMEM_LIMIT),
        name="ffn",
    )(*ffn_in)
    return out.reshape(x.shape)


def kernel(x, norm1_w, w_in, conv_w, conv_b, dt_bias, a_log, d_skip, ssd_norm_w, ret_norm_w,
           w_out, norm2_w, w_gate, w_up, w_down, final_norm_w):
    depth = w_in.shape[0]
    assert depth == 1
    return _layer(x, norm1_w[0], w_in[0], conv_w[0], conv_b[0], dt_bias[0], a_log[0], d_skip[0],
                  ssd_norm_w[0], ret_norm_w[0], w_out[0], norm2_w[0], w_gate[0], w_up[0],
                  w_down[0], final_norm_w)
```

```python
import functools
import math

import numpy as np
import jax
import jax.numpy as jnp
from jax import lax
from jax.experimental import pallas as pl
from jax.experimental.pallas import tpu as pltpu

F32 = jnp.float32
BF16 = jnp.bfloat16

D_MODEL = 1024
SSD_WIDTH = 1024
SSD_HEAD_DIM = 64
SSD_HEADS = 16
SSD_GROUPS = 2
SSD_HPG = SSD_HEADS // SSD_GROUPS
SSD_STATE = 128
GROUP_WIDTH = SSD_WIDTH // SSD_GROUPS
CONV_WIDTH = 4
BC_WIDTH = SSD_GROUPS * SSD_STATE
CONV_CH = SSD_WIDTH + 2 * BC_WIDTH
RET_WIDTH = 1024
RET_HEADS = 8
RET_PAIRS = RET_HEADS // 2
RET_V_DIM = 128
RET_QK_DIM = 64
QK_WIDTH = RET_HEADS * RET_QK_DIM
D_MIX = SSD_WIDTH + RET_WIDTH
CHUNK = 128
D_FF = 2816
ROPE_BASE = 10000.0
EPS = 1e-6
LOG2_E = math.log2(math.e)
LANES = 128
SUBLANES = 8
DT_PAD = LANES

PROJ_SIZES = [SSD_WIDTH, CONV_CH, SSD_HEADS, QK_WIDTH, QK_WIDTH, RET_WIDTH, RET_WIDTH]
PROJ_OFFS = [int(v) for v in np.cumsum([0] + PROJ_SIZES)]
_WIN_SIZES = [SSD_WIDTH, CONV_CH, DT_PAD, QK_WIDTH, QK_WIDTH, RET_WIDTH, RET_WIDTH]
WIN_OFFS = dict(zip(("z", "xbc", "dt", "q", "k", "v", "g"),
                    (int(v) for v in np.cumsum([0] + _WIN_SIZES[:-1]))))
WIN_WIDTH = sum(_WIN_SIZES)

MIX_BLOCK = 256
SET_REFS = 8
FFN_BLOCK = 512
PROJ_TILE = 256
CONV_TILE = 128
VMEM_LIMIT = 56 * 1024 * 1024

LOG_GAMMA = [math.log1p(-2.0 ** (-5.0 - h)) for h in range(RET_HEADS)]


def _dot(a, b):
    return jnp.dot(a.astype(BF16), b.astype(BF16), preferred_element_type=F32)


def _dot_nt(a, b):
    return lax.dot_general(a.astype(BF16), b.astype(BF16), (((1,), (1,)), ((), ())),
                           preferred_element_type=F32)


def _dot_tn(a, b):
    return lax.dot_general(a.astype(BF16), b.astype(BF16), (((0,), (0,)), ((), ())),
                           preferred_element_type=F32)


def _split(x, terms):
    parts = []
    r = x
    for i in range(terms):
        p = r.astype(BF16)
        parts.append(p)
        if i + 1 < terms:
            r = r - p.astype(F32)
    return jnp.concatenate(parts, axis=1)


def _silu(x):
    return x / (1.0 + jnp.exp(-x))


def _rmsnorm(x, w):
    return x * lax.rsqrt(jnp.mean(x * x, axis=-1, keepdims=True) + EPS) * w


class _Set:
    def __init__(self, refs):
        self.x, self.z, self.xa, self.dt, self.q, self.k, self.v, self.g = refs


def _mixer_kernel(nblk_seq, nblk_total,
                  x_ref, n1w_ref, win_ref,
                  convw_ref, convb_ref, dtb_ref, alog_ref, dskip_ref, ssdnw_ref, retnw_ref,
                  wout_ref, cos_ref, sin_ref,
                  out_ref, *scratch):
    set_a = _Set(scratch[0:SET_REFS])
    set_b = _Set(scratch[SET_REFS:2 * SET_REFS])
    (xbc_s, h_s, y_s, mixed_s, ssd_state, ret_state, tri_s, e2_s, qg_s, kg_s) = scratch[2 * SET_REFS:]
    blk = x_ref.shape[0]
    n_chunks = blk // CHUNK
    t = pl.program_id(0)

    @pl.when(t == 0)
    def _build_constants():
        row = lax.broadcasted_iota(jnp.int32, (CHUNK, CHUNK), 0)
        col = lax.broadcasted_iota(jnp.int32, (CHUNK, CHUNK), 1)
        tri_s[...] = jnp.where(col <= row, 1.0, 0.0).astype(BF16)
        er = lax.broadcasted_iota(jnp.int32, (2 * DT_PAD, SSD_WIDTH), 0) % DT_PAD
        ec = lax.broadcasted_iota(jnp.int32, (2 * DT_PAD, SSD_WIDTH), 1) // SSD_HEAD_DIM
        e2_s[...] = jnp.where(er == ec, 1.0, 0.0).astype(BF16)
        pos = lax.broadcasted_iota(jnp.int32, (CHUNK, LANES), 0).astype(F32)
        lane_head = (lax.broadcasted_iota(jnp.int32, (CHUNK, LANES), 1) // 32) % 2
        for p in range(RET_PAIRS):
            lg = jnp.where(lane_head == 0, LOG_GAMMA[2 * p], LOG_GAMMA[2 * p + 1])
            qg_s[p] = jnp.exp(pos * lg)
            kgam = jnp.exp(-pos * lg) * (RET_QK_DIM ** -0.5)
            for hh in range(2):
                kg_s[2 * p + hh] = jnp.where(lane_head == hh, kgam, 0.0)

    @pl.when(jnp.logical_and(t % nblk_seq == 0, t < nblk_total))
    def _reset_conv_carry():
        xbc_s[0:SUBLANES, :] = jnp.zeros((SUBLANES, CONV_CH), F32)

    @pl.when(t == 0)
    def _zero_first_scan_input():
        for ref in scratch[SET_REFS:2 * SET_REFS]:
            ref[...] = jnp.zeros(ref.shape, ref.dtype)

    @pl.when(jnp.logical_or(t == 0, (t + nblk_seq - 1) % nblk_seq == 0))
    def _reset_state():
        ssd_state[...] = jnp.zeros(ssd_state.shape, F32)
        ret_state[...] = jnp.zeros(ret_state.shape, F32)

    def project_units(dst):
        def norm():
            x = x_ref[...]
            dst.x[...] = x
            h_s[...] = _rmsnorm(x, n1w_ref[...]).astype(BF16)

        def proj(base, store, c0, c1):
            store(c0, c1, jnp.dot(h_s[...], win_ref[:, base + c0:base + c1], preferred_element_type=F32))

        def store_to(ref, row0=0):
            def store(c0, c1, val):
                ref[row0:row0 + blk, c0:c1] = val
            return store

        def conv(c0, c1):
            u = xbc_s[0:SUBLANES + blk, c0:c1]
            u1 = pltpu.roll(u, 1, axis=0)
            w = [convw_ref[tap:tap + 1, c0:c1] for tap in range(CONV_WIDTH)]
            near = convb_ref[:, c0:c1] + w[3] * u + w[2] * u1
            far = pltpu.roll(w[1] * u + w[0] * u1, 2, axis=0)
            dst.xa[:, c0:c1] = _silu((near + far)[SUBLANES:, :])

        def carry():
            xbc_s[0:SUBLANES, :] = xbc_s[blk:blk + SUBLANES, :]

        def tiles(width, step):
            return [(c, c + step) for c in range(0, width, step)]

        w_off = WIN_OFFS
        conv_in_dots = [functools.partial(proj, w_off["xbc"], store_to(xbc_s, SUBLANES), *cc)
                        for cc in tiles(CONV_CH, PROJ_TILE)]
        other_dots = []
        for name, ref in (("z", dst.z), ("dt", dst.dt), ("q", dst.q), ("k", dst.k),
                          ("v", dst.v), ("g", dst.g)):
            width = ref.shape[1]
            other_dots += [functools.partial(proj, w_off[name], store_to(ref), *cc)
                           for cc in tiles(width, min(width, PROJ_TILE))]
        convs = [functools.partial(conv, *cc) for cc in tiles(CONV_CH, CONV_TILE)]
        return norm, conv_in_dots, other_dots, convs + [carry]

    a_neg2 = -jnp.exp(alog_ref[...]) * LOG2_E
    lane = lax.broadcasted_iota(jnp.int32, (1, LANES), 1)
    lane_ok = lane < SSD_HEADS
    low_half = lane < SSD_HEAD_DIM
    row_i = lax.broadcasted_iota(jnp.int32, (CHUNK, CHUNK), 0)
    col_i = lax.broadcasted_iota(jnp.int32, (CHUNK, CHUNK), 1)
    causal = col_i <= row_i

    def chunk_units(src, c):
        rows = slice(c * CHUNK, (c + 1) * CHUNK)
        env = {}

        def ssd_prep():
            dt_pre = src.dt[rows, :] + dtb_ref[...]
            dt = jnp.maximum(dt_pre, 0.0) + jnp.log1p(jnp.exp(-jnp.abs(dt_pre)))
            dt = jnp.where(lane_ok, dt, 0.0)
            dta = dt * a_neg2
            cs3 = jnp.dot(tri_s[...], _split(dta, 3), preferred_element_type=F32)
            cs = cs3[:, 0:DT_PAD] + cs3[:, DT_PAD:2 * DT_PAD] + cs3[:, 2 * DT_PAD:3 * DT_PAD]
            env["cs"] = cs
            env["csd_t"] = (cs - jnp.log2(dt)).T
            cs_last = cs[CHUNK - 1:CHUNK, :]
            ecs_e = jnp.dot(_split(jnp.exp2(cs), 2), e2_s[...], preferred_element_type=F32)
            dtd_e = jnp.dot(_split(dt * jnp.exp2(cs_last - cs), 2), e2_s[...], preferred_element_type=F32)
            env["ecs_e"] = ecs_e
            env["chunk_decay"] = ecs_e[CHUNK - 1:CHUNK, :]
            env["xdtd_b"] = (src.xa[rows, 0:SSD_WIDTH] * dtd_e).astype(BF16)

        def ssd_group(g):
            gs = slice(g * GROUP_WIDTH, (g + 1) * GROUP_WIDTH)
            bg = src.xa[rows, SSD_WIDTH + g * SSD_STATE:SSD_WIDTH + (g + 1) * SSD_STATE].astype(BF16)
            cg = src.xa[rows, SSD_WIDTH + BC_WIDTH + g * SSD_STATE:
                        SSD_WIDTH + BC_WIDTH + (g + 1) * SSD_STATE].astype(BF16)
            env["cb"] = _dot_nt(cg, bg)
            prev = ssd_state[g]
            env["y_off"] = _dot(cg, prev) * env["ecs_e"][:, gs]
            ssd_state[g] = prev * env["chunk_decay"][:, gs] + _dot_tn(bg, env["xdtd_b"][:, gs])

        def ssd_tile(g, tl):
            ts = slice(g * GROUP_WIDTH + tl * LANES, g * GROUP_WIDTH + (tl + 1) * LANES)
            cs, csd_t, cb = env["cs"], env["csd_t"], env["cb"]
            ms = []
            for hh in range(2):
                hd = g * SSD_HPG + 2 * tl + hh
                seg = cs[:, hd:hd + 1] - csd_t[hd:hd + 1, :]
                decay_dt = jnp.exp2(jnp.where(causal, seg, -jnp.inf))
                ms.append((cb * decay_dt).astype(BF16))
            xt = src.xa[rows, ts]
            rhs = jnp.concatenate([jnp.where(low_half, xt, 0.0).astype(BF16),
                                   jnp.where(low_half, 0.0, xt).astype(BF16)], axis=0)
            y = jnp.dot(jnp.concatenate(ms, axis=1), rhs, preferred_element_type=F32)
            y = y + env["y_off"][:, tl * LANES:(tl + 1) * LANES] + src.xa[rows, ts] * dskip_ref[:, ts]
            y_s[:, ts] = y * _silu(src.z[rows, ts])

        def ssd_norm(g):
            gs = slice(g * GROUP_WIDTH, (g + 1) * GROUP_WIDTH)
            yg = y_s[:, gs]
            yg = yg * lax.rsqrt(jnp.mean(yg * yg, axis=-1, keepdims=True) + EPS)
            mixed_s[rows, gs] = (yg * ssdnw_ref[:, gs]).astype(BF16)

        def ret_pair(p):
            ts = slice(p * LANES, (p + 1) * LANES)
            cos = cos_ref[rows, :]
            sin = sin_ref[rows, :]
            qt = src.q[rows, ts]
            kt = src.k[rows, ts]
            qg = ((qt * cos + pltpu.roll(qt, 64, axis=1) * sin) * qg_s[p]).astype(BF16)
            kr = kt * cos + pltpu.roll(kt, 64, axis=1) * sin
            kg = [(kr * kg_s[2 * p + hh]).astype(BF16) for hh in range(2)]
            env["qg"], env["kg"] = qg, kg
            env["scores"] = _dot_nt(qg, jnp.concatenate(kg, axis=0))

        def ret_head(p, hh):
            hd = 2 * p + hh
            qg, kg, scores = env["qg"], env["kg"], env["scores"]
            vs = slice(hd * RET_V_DIM, (hd + 1) * RET_V_DIM)
            v_h = src.v[rows, vs].astype(BF16)
            probs = jnp.where(causal, scores[:, hh * CHUNK:(hh + 1) * CHUNK], 0.0).astype(BF16)
            u_prev = ret_state[hd]
            yr = jnp.dot(jnp.concatenate([probs, qg], axis=1),
                         jnp.concatenate([v_h, u_prev.astype(BF16)], axis=0),
                         preferred_element_type=F32)
            ret_state[hd] = (u_prev + _dot_tn(kg[hh], v_h)) * math.exp(CHUNK * LOG_GAMMA[hd])
            mu = jnp.mean(yr, axis=-1, keepdims=True)
            yc = yr - mu
            var = jnp.mean(yc * yc, axis=-1, keepdims=True)
            yn = yc * lax.rsqrt(var + EPS)
            yn = yn * retnw_ref[:, vs] * _silu(src.g[rows, vs])
            mixed_s[rows, SSD_WIDTH + hd * RET_V_DIM:SSD_WIDTH + (hd + 1) * RET_V_DIM] = yn.astype(BF16)

        ssd = []
        for g in range(SSD_GROUPS):
            ssd.append(functools.partial(ssd_group, g))
            ssd += [functools.partial(ssd_tile, g, tl) for tl in range(GROUP_WIDTH // LANES)]
            ssd.append(functools.partial(ssd_norm, g))
        ret = []
        for p in range(RET_PAIRS):
            ret.append(functools.partial(ret_pair, p))
            ret += [functools.partial(ret_head, p, hh) for hh in range(2)]
        return ssd_prep, ssd, ret

    def scan_units(src):
        chunks = [chunk_units(src, c) for c in range(n_chunks)]
        units = [prep for prep, _, _ in chunks]
        for _, ssd, ret in chunks:
            units += interleave(ssd, ret)
        return units

    def out_proj_units(src):
        def out_proj(c0, c1):
            out_ref[:, c0:c1] = src.x[:, c0:c1] + jnp.dot(mixed_s[...], wout_ref[:, c0:c1],
                                                           preferred_element_type=F32)
        return [functools.partial(out_proj, c0, c0 + PROJ_TILE) for c0 in range(0, D_MODEL, PROJ_TILE)]

    def interleave(a, b):
        na, nb = len(a), len(b)
        merged = []
        ia = ib = 0
        while ia < na or ib < nb:
            if ib >= nb or (ia < na and ia * nb <= ib * na):
                merged.append(a[ia])
                ia += 1
            else:
                merged.append(b[ib])
                ib += 1
        return merged

    def step(src, dst):
        norm, conv_in_dots, other_dots, convs = project_units(dst)
        norm()
        for unit in (interleave(scan_units(src), conv_in_dots + other_dots)
                     + interleave(convs, out_proj_units(src))):
            unit()

    @pl.when(t % 2 == 1)
    def _odd():
        step(set_a, set_b)

    @pl.when(t % 2 == 0)
    def _even():
        step(set_b, set_a)


def _ffn_kernel(x_ref, n2w_ref, wg_ref, wu_ref, wd_ref, fnw_ref, out_ref):
    x = x_ref[...]
    h = _rmsnorm(x, n2w_ref[...]).astype(BF16)
    gate = jnp.dot(h, wg_ref[...], preferred_element_type=F32)
    up = jnp.dot(h, wu_ref[...], preferred_element_type=F32)
    act = (_silu(gate) * up).astype(BF16)
    y = x + jnp.dot(act, wd_ref[...], preferred_element_type=F32)
    out_ref[...] = _rmsnorm(y, fnw_ref[...])


def _resident(shape):
    nd = len(shape)
    return pl.BlockSpec(shape, lambda *_: (0,) * nd, pipeline_mode=pl.Buffered(1))


def _qk_perm():
    half = RET_QK_DIM // 2
    idx = []
    for p in range(RET_PAIRS):
        for part in range(2):
            for hh in range(2):
                base = (2 * p + hh) * RET_QK_DIM + part * half
                idx.extend(range(base, base + half))
    return np.asarray(idx, dtype=np.int32)


def _rope_tables(seqlen):
    half = RET_QK_DIM // 2
    inv_freq = ROPE_BASE ** (-np.arange(half, dtype=np.float64) / half)
    ang = np.arange(seqlen, dtype=np.float64)[:, None] * inv_freq[None, :]
    cos = np.cos(ang)
    sin = np.sin(ang)
    return (jnp.asarray(np.concatenate([cos, cos, cos, cos], axis=1), dtype=F32),
            jnp.asarray(np.concatenate([-sin, -sin, sin, sin], axis=1), dtype=F32))


def _layer(x, norm1_w, w_in, conv_w, conv_b, dt_bias, a_log, d_skip, ssd_norm_w,
           ret_norm_w, w_out, norm2_w, w_gate, w_up, w_down, out_norm_w):
    bsz, seqlen, _ = x.shape
    tokens = bsz * seqlen
    blk = MIX_BLOCK
    nblk_seq = seqlen // blk
    nblk_total = tokens // blk
    o = PROJ_OFFS
    perm = _qk_perm()
    w_all = jnp.concatenate([
        w_in[:, o[0]:o[1]], w_in[:, o[1]:o[2]],
        jnp.pad(w_in[:, o[2]:o[3]], ((0, 0), (0, DT_PAD - SSD_HEADS))),
        w_in[:, o[3]:o[4]][:, perm], w_in[:, o[4]:o[5]][:, perm],
        w_in[:, o[5]:o[6]], w_in[:, o[6]:o[7]]], axis=1).astype(BF16)
    assert w_all.shape[1] == WIN_WIDTH
    pad16 = lambda v: jnp.pad(v.astype(F32)[None, :], ((0, 0), (0, DT_PAD - SSD_HEADS)))
    cos_t, sin_t = _rope_tables(seqlen)
    row = lambda v: v.astype(F32)[None, :]

    mixer_in = [
        x.reshape(tokens, D_MODEL), row(norm1_w), w_all,
        conv_w.astype(F32), row(conv_b), pad16(dt_bias), pad16(a_log),
        row(jnp.repeat(d_skip, SSD_HEAD_DIM)), row(ssd_norm_w), row(ret_norm_w),
        w_out.astype(BF16), cos_t, sin_t,
    ]
    last = nblk_total - 1
    in_specs = [pl.BlockSpec((blk, D_MODEL), lambda t: (jnp.minimum(t, last), 0))]
    in_specs += [_resident(a.shape) for a in mixer_in[1:-2]]
    rope_spec = pl.BlockSpec((blk, LANES), lambda t: (jnp.clip(t - 1, 0, last) % nblk_seq, 0))
    in_specs += [rope_spec, rope_spec]
    act_set = [
        pltpu.VMEM((blk, D_MODEL), F32),
        pltpu.VMEM((blk, SSD_WIDTH), F32),
        pltpu.VMEM((blk, CONV_CH), F32),
        pltpu.VMEM((blk, DT_PAD), F32),
        pltpu.VMEM((blk, QK_WIDTH), F32),
        pltpu.VMEM((blk, QK_WIDTH), F32),
        pltpu.VMEM((blk, RET_WIDTH), F32),
        pltpu.VMEM((blk, RET_WIDTH), F32),
    ]
    assert len(act_set) == SET_REFS
    scratch = act_set + act_set + [
        pltpu.VMEM((blk + 2 * SUBLANES, CONV_CH), F32),
        pltpu.VMEM((blk, D_MODEL), BF16),
        pltpu.VMEM((CHUNK, SSD_WIDTH), F32),
        pltpu.VMEM((blk, D_MIX), BF16),
        pltpu.VMEM((SSD_GROUPS, SSD_STATE, GROUP_WIDTH), F32),
        pltpu.VMEM((RET_HEADS, LANES, RET_V_DIM), F32),
        pltpu.VMEM((CHUNK, CHUNK), BF16),
        pltpu.VMEM((2 * DT_PAD, SSD_WIDTH), BF16),
        pltpu.VMEM((RET_PAIRS, CHUNK, LANES), F32),
        pltpu.VMEM((RET_HEADS, CHUNK, LANES), F32),
    ]
    x1 = pl.pallas_call(
        functools.partial(_mixer_kernel, nblk_seq, nblk_total),
        grid=(nblk_total + 1,),
        in_specs=in_specs,
        out_specs=pl.BlockSpec((blk, D_MODEL), lambda t: (jnp.maximum(t - 1, 0), 0)),
        out_shape=jax.ShapeDtypeStruct((tokens, D_MODEL), F32),
        scratch_shapes=scratch,
        compiler_params=pltpu.CompilerParams(
            dimension_semantics=("arbitrary",), vmem_limit_bytes=VMEM_LIMIT),
        name="mixer",
    )(*mixer_in)

    fblk = FFN_BLOCK
    ffn_in = [x1, row(norm2_w), w_gate.astype(BF16), w_up.astype(BF16),
              w_down.astype(BF16), row(out_norm_w)]
    out = pl.pallas_call(
        _ffn_kernel,
        grid=(tokens // fblk,),
        in_specs=[pl.BlockSpec((fblk, D_MODEL), lambda i: (i, 0))] + [_resident(a.shape) for a in ffn_in[1:]],
        out_specs=pl.BlockSpec((fblk, D_MODEL), lambda i: (i, 0)),
        out_shape=jax.ShapeDtypeStruct((tokens, D_MODEL), F32),
        compiler_params=pltpu.CompilerParams(
            dimension_semantics=("arbitrary",), vmem_limit_bytes=VMEM_LIMIT),
        name="ffn",
    )(*ffn_in)
    return out.reshape(x.shape)


def kernel(x, norm1_w, w_in, conv_w, conv_b, dt_bias, a_log, d_skip, ssd_norm_w, ret_norm_w,
           w_out, norm2_w, w_gate, w_up, w_down, final_norm_w):
    depth = w_in.shape[0]
    assert depth == 1
    return _layer(x, norm1_w[0], w_in[0], conv_w[0], conv_b[0], dt_bias[0], a_log[0], d_skip[0],
                  ssd_norm_w[0], ret_norm_w[0], w_out[0], norm2_w[0], w_gate[0], w_up[0],
                  w_down[0], final_norm_w)
```

```python
import functools
import math

import numpy as np
import jax
import jax.numpy as jnp
from jax import lax
from jax.experimental import pallas as pl
from jax.experimental.pallas import tpu as pltpu

F32 = jnp.float32
BF16 = jnp.bfloat16

D_MODEL = 1024
SSD_WIDTH = 1024
SSD_HEAD_DIM = 64
SSD_HEADS = 16
SSD_GROUPS = 2
SSD_HPG = SSD_HEADS // SSD_GROUPS
SSD_STATE = 128
GROUP_WIDTH = SSD_WIDTH // SSD_GROUPS
CONV_WIDTH = 4
BC_WIDTH = SSD_GROUPS * SSD_STATE
CONV_CH = SSD_WIDTH + 2 * BC_WIDTH
RET_WIDTH = 1024
RET_HEADS = 8
RET_PAIRS = RET_HEADS // 2
RET_V_DIM = 128
RET_QK_DIM = 64
QK_WIDTH = RET_HEADS * RET_QK_DIM
D_MIX = SSD_WIDTH + RET_WIDTH
CHUNK = 128
D_FF = 2816
ROPE_BASE = 10000.0
EPS = 1e-6
LOG2_E = math.log2(math.e)
LANES = 128
SUBLANES = 8
DT_PAD = LANES

PROJ_SIZES = [SSD_WIDTH, CONV_CH, SSD_HEADS, QK_WIDTH, QK_WIDTH, RET_WIDTH, RET_WIDTH]
PROJ_OFFS = [int(v) for v in np.cumsum([0] + PROJ_SIZES)]
_WIN_SIZES = [SSD_WIDTH, CONV_CH, DT_PAD, QK_WIDTH, QK_WIDTH, RET_WIDTH, RET_WIDTH]
WIN_OFFS = dict(zip(("z", "xbc", "dt", "q", "k", "v", "g"),
                    (int(v) for v in np.cumsum([0] + _WIN_SIZES[:-1]))))
WIN_WIDTH = sum(_WIN_SIZES)

MIX_BLOCK = 256
SET_REFS = 8
FFN_BLOCK = 512
PROJ_TILE = 256
CONV_TILE = 128
VMEM_LIMIT = 56 * 1024 * 1024

LOG_GAMMA = [math.log1p(-2.0 ** (-5.0 - h)) for h in range(RET_HEADS)]


def _dot(a, b):
    return jnp.dot(a.astype(BF16), b.astype(BF16), preferred_element_type=F32)


def _dot_nt(a, b):
    return lax.dot_general(a.astype(BF16), b.astype(BF16), (((1,), (1,)), ((), ())),
                           preferred_element_type=F32)


def _dot_tn(a, b):
    return lax.dot_general(a.astype(BF16), b.astype(BF16), (((0,), (0,)), ((), ())),
                           preferred_element_type=F32)


def _split(x, terms):
    parts = []
    r = x
    for i in range(terms):
        p = r.astype(BF16)
        parts.append(p)
        if i + 1 < terms:
            r = r - p.astype(F32)
    return jnp.concatenate(parts, axis=1)


def _silu(x):
    return x / (1.0 + jnp.exp(-x))


def _rmsnorm(x, w):
    return x * lax.rsqrt(jnp.mean(x * x, axis=-1, keepdims=True) + EPS) * w


class _Set:
    def __init__(self, refs):
        self.x, self.z, self.xa, self.dt, self.q, self.k, self.v, self.g = refs


def _mixer_kernel(nblk_seq, nblk_total,
                  x_ref, n1w_ref, win_ref,
                  convw_ref, convb_ref, dtb_ref, alog_ref, dskip_ref, ssdnw_ref, retnw_ref,
                  wout_ref, cos_ref, sin_ref,
                  out_ref, *scratch):
    set_a = _Set(scratch[0:SET_REFS])
    set_b = _Set(scratch[SET_REFS:2 * SET_REFS])
    (xbc_s, h_s, y_s, mixed_s, ssd_state, ret_state, tri_s, e2_s, qg_s, kg_s) = scratch[2 * SET_REFS:]
    blk = x_ref.shape[0]
    n_chunks = blk // CHUNK
    t = pl.program_id(0)

    @pl.when(t == 0)
    def _build_constants():
        row = lax.broadcasted_iota(jnp.int32, (CHUNK, CHUNK), 0)
        col = lax.broadcasted_iota(jnp.int32, (CHUNK, CHUNK), 1)
        tri_s[...] = jnp.where(col <= row, 1.0, 0.0).astype(BF16)
        er = lax.broadcasted_iota(jnp.int32, (2 * DT_PAD, SSD_WIDTH), 0) % DT_PAD
        ec = lax.broadcasted_iota(jnp.int32, (2 * DT_PAD, SSD_WIDTH), 1) // SSD_HEAD_DIM
        e2_s[...] = jnp.where(er == ec, 1.0, 0.0).astype(BF16)
        pos = lax.broadcasted_iota(jnp.int32, (CHUNK, LANES), 0).astype(F32)
        lane_head = (lax.broadcasted_iota(jnp.int32, (CHUNK, LANES), 1) // 32) % 2
        for p in range(RET_PAIRS):
            lg = jnp.where(lane_head == 0, LOG_GAMMA[2 * p], LOG_GAMMA[2 * p + 1])
            qg_s[p] = jnp.exp(pos * lg)
            kgam = jnp.exp(-pos * lg) * (RET_QK_DIM ** -0.5)
            for hh in range(2):
                kg_s[2 * p + hh] = jnp.where(lane_head == hh, kgam, 0.0)

    @pl.when(jnp.logical_and(t % nblk_seq == 0, t < nblk_total))
    def _reset_conv_carry():
        xbc_s[0:SUBLANES, :] = jnp.zeros((SUBLANES, CONV_CH), F32)

    @pl.when(t == 0)
    def _zero_first_scan_input():
        for ref in scratch[SET_REFS:2 * SET_REFS]:
            ref[...] = jnp.zeros(ref.shape, ref.dtype)

    @pl.when(jnp.logical_or(t == 0, (t + nblk_seq - 1) % nblk_seq == 0))
    def _reset_state():
        ssd_state[...] = jnp.zeros(ssd_state.shape, F32)
        ret_state[...] = jnp.zeros(ret_state.shape, F32)

    def project_units(dst):
        def norm():
            x = x_ref[...]
            dst.x[...] = x
            h_s[...] = _rmsnorm(x, n1w_ref[...]).astype(BF16)

        def proj(base, store, c0, c1):
            store(c0, c1, jnp.dot(h_s[...], win_ref[:, base + c0:base + c1], preferred_element_type=F32))

        def store_to(ref, row0=0):
            def store(c0, c1, val):
                ref[row0:row0 + blk, c0:c1] = val.astype(ref.dtype)
            return store

        def conv(c0, c1):
            u = xbc_s[0:SUBLANES + blk, c0:c1]
            u1 = pltpu.roll(u, 1, axis=0)
            w = [convw_ref[tap:tap + 1, c0:c1] for tap in range(CONV_WIDTH)]
            near = convb_ref[:, c0:c1] + w[3] * u + w[2] * u1
            far = pltpu.roll(w[1] * u + w[0] * u1, 2, axis=0)
            dst.xa[:, c0:c1] = _silu((near + far)[SUBLANES:, :])

        def carry():
            xbc_s[0:SUBLANES, :] = xbc_s[blk:blk + SUBLANES, :]

        def tiles(width, step):
            return [(c, c + step) for c in range(0, width, step)]

        w_off = WIN_OFFS
        conv_in_dots = [functools.partial(proj, w_off["xbc"], store_to(xbc_s, SUBLANES), *cc)
                        for cc in tiles(CONV_CH, PROJ_TILE)]
        other_dots = []
        for name, ref in (("z", dst.z), ("dt", dst.dt), ("q", dst.q), ("k", dst.k),
                          ("v", dst.v), ("g", dst.g)):
            width = ref.shape[1]
            other_dots += [functools.partial(proj, w_off[name], store_to(ref), *cc)
                           for cc in tiles(width, min(width, PROJ_TILE))]
        convs = [functools.partial(conv, *cc) for cc in tiles(CONV_CH, CONV_TILE)]
        return norm, conv_in_dots, other_dots, convs + [carry]

    a_neg2 = -jnp.exp(alog_ref[...]) * LOG2_E
    lane = lax.broadcasted_iota(jnp.int32, (1, LANES), 1)
    lane_ok = lane < SSD_HEADS
    low_half = lane < SSD_HEAD_DIM
    row_i = lax.broadcasted_iota(jnp.int32, (CHUNK, CHUNK), 0)
    col_i = lax.broadcasted_iota(jnp.int32, (CHUNK, CHUNK), 1)
    causal = col_i <= row_i

    def chunk_units(src, c):
        rows = slice(c * CHUNK, (c + 1) * CHUNK)
        env = {}

        def ssd_prep():
            dt_pre = src.dt[rows, :] + dtb_ref[...]
            dt = jnp.maximum(dt_pre, 0.0) + jnp.log1p(jnp.exp(-jnp.abs(dt_pre)))
            dt = jnp.where(lane_ok, dt, 0.0)
            dta = dt * a_neg2
            cs3 = jnp.dot(tri_s[...], _split(dta, 3), preferred_element_type=F32)
            cs = cs3[:, 0:DT_PAD] + cs3[:, DT_PAD:2 * DT_PAD] + cs3[:, 2 * DT_PAD:3 * DT_PAD]
            env["cs"] = cs
            env["csd_t"] = (cs - jnp.log2(dt)).T
            cs_last = cs[CHUNK - 1:CHUNK, :]
            ecs_e = jnp.dot(_split(jnp.exp2(cs), 2), e2_s[...], preferred_element_type=F32)
            dtd_e = jnp.dot(_split(dt * jnp.exp2(cs_last - cs), 2), e2_s[...], preferred_element_type=F32)
            env["ecs_e"] = ecs_e
            env["chunk_decay"] = ecs_e[CHUNK - 1:CHUNK, :]
            env["xdtd_b"] = (src.xa[rows, 0:SSD_WIDTH] * dtd_e).astype(BF16)

        def ssd_group(g):
            gs = slice(g * GROUP_WIDTH, (g + 1) * GROUP_WIDTH)
            bg = src.xa[rows, SSD_WIDTH + g * SSD_STATE:SSD_WIDTH + (g + 1) * SSD_STATE].astype(BF16)
            cg = src.xa[rows, SSD_WIDTH + BC_WIDTH + g * SSD_STATE:
                        SSD_WIDTH + BC_WIDTH + (g + 1) * SSD_STATE].astype(BF16)
            env["cb"] = _dot_nt(cg, bg)
            prev = ssd_state[g]
            env["y_off"] = _dot(cg, prev) * env["ecs_e"][:, gs]
            ssd_state[g] = prev * env["chunk_decay"][:, gs] + _dot_tn(bg, env["xdtd_b"][:, gs])

        def ssd_tile(g, tl):
            ts = slice(g * GROUP_WIDTH + tl * LANES, g * GROUP_WIDTH + (tl + 1) * LANES)
            cs, csd_t, cb = env["cs"], env["csd_t"], env["cb"]
            ms = []
            for hh in range(2):
                hd = g * SSD_HPG + 2 * tl + hh
                seg = cs[:, hd:hd + 1] - csd_t[hd:hd + 1, :]
                decay_dt = jnp.exp2(jnp.where(causal, seg, -jnp.inf))
                ms.append((cb * decay_dt).astype(BF16))
            xt = src.xa[rows, ts].astype(BF16)
            zero = jnp.zeros_like(xt)
            rhs = jnp.concatenate([jnp.where(low_half, xt, zero), jnp.where(low_half, zero, xt)], axis=0)
            y = jnp.dot(jnp.concatenate(ms, axis=1), rhs, preferred_element_type=F32)
            y = y + env["y_off"][:, tl * LANES:(tl + 1) * LANES] + src.xa[rows, ts] * dskip_ref[:, ts]
            y_s[:, ts] = y * _silu(src.z[rows, ts])

        def ssd_norm(g):
            gs = slice(g * GROUP_WIDTH, (g + 1) * GROUP_WIDTH)
            yg = y_s[:, gs]
            yg = yg * lax.rsqrt(jnp.mean(yg * yg, axis=-1, keepdims=True) + EPS)
            mixed_s[rows, gs] = (yg * ssdnw_ref[:, gs]).astype(BF16)

        def ret_pair(p):
            ts = slice(p * LANES, (p + 1) * LANES)
            cos = cos_ref[rows, :]
            sin = sin_ref[rows, :]
            qt = src.q[rows, ts]
            kt = src.k[rows, ts]
            qg = ((qt * cos + pltpu.roll(qt, 64, axis=1) * sin) * qg_s[p]).astype(BF16)
            kr = kt * cos + pltpu.roll(kt, 64, axis=1) * sin
            kg = [(kr * kg_s[2 * p + hh]).astype(BF16) for hh in range(2)]
            env["qg"], env["kg"] = qg, kg
            env["scores"] = _dot_nt(qg, jnp.concatenate(kg, axis=0))

        def ret_head(p, hh):
            hd = 2 * p + hh
            qg, kg, scores = env["qg"], env["kg"], env["scores"]
            vs = slice(hd * RET_V_DIM, (hd + 1) * RET_V_DIM)
            v_h = src.v[rows, vs]
            probs = jnp.where(causal, scores[:, hh * CHUNK:(hh + 1) * CHUNK], 0.0).astype(BF16)
            u_prev = ret_state[hd]
            yr = jnp.dot(jnp.concatenate([probs, qg], axis=1),
                         jnp.concatenate([v_h, u_prev.astype(BF16)], axis=0),
                         preferred_element_type=F32)
            ret_state[hd] = (u_prev + _dot_tn(kg[hh], v_h)) * math.exp(CHUNK * LOG_GAMMA[hd])
            mu = jnp.mean(yr, axis=-1, keepdims=True)
            yc = yr - mu
            var = jnp.mean(yc * yc, axis=-1, keepdims=True)
            yn = yc * lax.rsqrt(var + EPS)
            yn = yn * retnw_ref[:, vs] * _silu(src.g[rows, vs])
            mixed_s[rows, SSD_WIDTH + hd * RET_V_DIM:SSD_WIDTH + (hd + 1) * RET_V_DIM] = yn.astype(BF16)

        ssd = []
        for g in range(SSD_GROUPS):
            ssd.append(functools.partial(ssd_group, g))
            ssd += [functools.partial(ssd_tile, g, tl) for tl in range(GROUP_WIDTH // LANES)]
            ssd.append(functools.partial(ssd_norm, g))
        ret = []
        for p in range(RET_PAIRS):
            ret.append(functools.partial(ret_pair, p))
            ret += [functools.partial(ret_head, p, hh) for hh in range(2)]
        return ssd_prep, ssd, ret

    def scan_units(src):
        chunks = [chunk_units(src, c) for c in range(n_chunks)]
        units = [prep for prep, _, _ in chunks]
        for c, (_, ssd, ret) in enumerate(chunks):
            units += interleave(ssd, ret)
            if c + 1 < n_chunks:
                units += out_proj_units(src, c)
        return units

    def out_proj_units(src, c):
        rows = slice(c * CHUNK, (c + 1) * CHUNK)

        def out_proj(c0, c1):
            out_ref[rows, c0:c1] = src.x[rows, c0:c1] + jnp.dot(mixed_s[rows, :], wout_ref[:, c0:c1],
                                                                 preferred_element_type=F32)
        return [functools.partial(out_proj, c0, c0 + PROJ_TILE) for c0 in range(0, D_MODEL, PROJ_TILE)]

    def interleave(a, b):
        na, nb = len(a), len(b)
        merged = []
        ia = ib = 0
        while ia < na or ib < nb:
            if ib >= nb or (ia < na and ia * nb <= ib * na):
                merged.append(a[ia])
                ia += 1
            else:
                merged.append(b[ib])
                ib += 1
        return merged

    def step(src, dst):
        norm, conv_in_dots, other_dots, convs = project_units(dst)
        norm()
        for unit in (interleave(scan_units(src), conv_in_dots + other_dots)
                     + interleave(convs, out_proj_units(src, n_chunks - 1))):
            unit()

    @pl.when(t % 2 == 1)
    def _odd():
        step(set_a, set_b)

    @pl.when(t % 2 == 0)
    def _even():
        step(set_b, set_a)


def _ffn_kernel(x_ref, n2w_ref, wg_ref, wu_ref, wd_ref, fnw_ref, out_ref):
    x = x_ref[...]
    h = _rmsnorm(x, n2w_ref[...]).astype(BF16)
    gate = jnp.dot(h, wg_ref[...], preferred_element_type=F32)
    up = jnp.dot(h, wu_ref[...], preferred_element_type=F32)
    act = (_silu(gate) * up).astype(BF16)
    y = x + jnp.dot(act, wd_ref[...], preferred_element_type=F32)
    out_ref[...] = _rmsnorm(y, fnw_ref[...])


def _resident(shape):
    nd = len(shape)
    return pl.BlockSpec(shape, lambda *_: (0,) * nd, pipeline_mode=pl.Buffered(1))


def _qk_perm():
    half = RET_QK_DIM // 2
    idx = []
    for p in range(RET_PAIRS):
        for part in range(2):
            for hh in range(2):
                base = (2 * p + hh) * RET_QK_DIM + part * half
                idx.extend(range(base, base + half))
    return np.asarray(idx, dtype=np.int32)


def _rope_tables(seqlen):
    half = RET_QK_DIM // 2
    inv_freq = ROPE_BASE ** (-np.arange(half, dtype=np.float64) / half)
    ang = np.arange(seqlen, dtype=np.float64)[:, None] * inv_freq[None, :]
    cos = np.cos(ang)
    sin = np.sin(ang)
    return (jnp.asarray(np.concatenate([cos, cos, cos, cos], axis=1), dtype=F32),
            jnp.asarray(np.concatenate([-sin, -sin, sin, sin], axis=1), dtype=F32))


def _layer(x, norm1_w, w_in, conv_w, conv_b, dt_bias, a_log, d_skip, ssd_norm_w,
           ret_norm_w, w_out, norm2_w, w_gate, w_up, w_down, out_norm_w):
    bsz, seqlen, _ = x.shape
    tokens = bsz * seqlen
    blk = MIX_BLOCK
    nblk_seq = seqlen // blk
    nblk_total = tokens // blk
    o = PROJ_OFFS
    perm = _qk_perm()
    w_all = jnp.concatenate([
        w_in[:, o[0]:o[1]], w_in[:, o[1]:o[2]],
        jnp.pad(w_in[:, o[2]:o[3]], ((0, 0), (0, DT_PAD - SSD_HEADS))),
        w_in[:, o[3]:o[4]][:, perm], w_in[:, o[4]:o[5]][:, perm],
        w_in[:, o[5]:o[6]], w_in[:, o[6]:o[7]]], axis=1).astype(BF16)
    assert w_all.shape[1] == WIN_WIDTH
    pad16 = lambda v: jnp.pad(v.astype(F32)[None, :], ((0, 0), (0, DT_PAD - SSD_HEADS)))
    cos_t, sin_t = _rope_tables(seqlen)
    row = lambda v: v.astype(F32)[None, :]

    mixer_in = [
        x.reshape(tokens, D_MODEL), row(norm1_w), w_all,
        conv_w.astype(F32), row(conv_b), pad16(dt_bias), pad16(a_log),
        row(jnp.repeat(d_skip, SSD_HEAD_DIM)), row(ssd_norm_w), row(ret_norm_w),
        w_out.astype(BF16), cos_t, sin_t,
    ]
    last = nblk_total - 1
    in_specs = [pl.BlockSpec((blk, D_MODEL), lambda t: (jnp.minimum(t, last), 0))]
    in_specs += [_resident(a.shape) for a in mixer_in[1:-2]]
    rope_spec = pl.BlockSpec((blk, LANES), lambda t: (jnp.clip(t - 1, 0, last) % nblk_seq, 0))
    in_specs += [rope_spec, rope_spec]
    act_set = [
        pltpu.VMEM((blk, D_MODEL), F32),
        pltpu.VMEM((blk, SSD_WIDTH), F32),
        pltpu.VMEM((blk, CONV_CH), F32),
        pltpu.VMEM((blk, DT_PAD), F32),
        pltpu.VMEM((blk, QK_WIDTH), F32),
        pltpu.VMEM((blk, QK_WIDTH), F32),
        pltpu.VMEM((blk, RET_WIDTH), BF16),
        pltpu.VMEM((blk, RET_WIDTH), F32),
    ]
    assert len(act_set) == SET_REFS
    scratch = act_set + act_set + [
        pltpu.VMEM((blk + 2 * SUBLANES, CONV_CH), F32),
        pltpu.VMEM((blk, D_MODEL), BF16),
        pltpu.VMEM((CHUNK, SSD_WIDTH), F32),
        pltpu.VMEM((blk, D_MIX), BF16),
        pltpu.VMEM((SSD_GROUPS, SSD_STATE, GROUP_WIDTH), F32),
        pltpu.VMEM((RET_HEADS, LANES, RET_V_DIM), F32),
        pltpu.VMEM((CHUNK, CHUNK), BF16),
        pltpu.VMEM((2 * DT_PAD, SSD_WIDTH), BF16),
        pltpu.VMEM((RET_PAIRS, CHUNK, LANES), F32),
        pltpu.VMEM((RET_HEADS, CHUNK, LANES), F32),
    ]
    x1 = pl.pallas_call(
        functools.partial(_mixer_kernel, nblk_seq, nblk_total),
        grid=(nblk_total + 1,),
        in_specs=in_specs,
        out_specs=pl.BlockSpec((blk, D_MODEL), lambda t: (jnp.maximum(t - 1, 0), 0)),
        out_shape=jax.ShapeDtypeStruct((tokens, D_MODEL), F32),
        scratch_shapes=scratch,
        compiler_params=pltpu.CompilerParams(
            dimension_semantics=("arbitrary",), vmem_limit_bytes=VMEM_LIMIT),
        name="mixer",
    )(*mixer_in)

    fblk = FFN_BLOCK
    ffn_in = [x1, row(norm2_w), w_gate.astype(BF16), w_up.astype(BF16),
              w_down.astype(BF16), row(out_norm_w)]
    out = pl.pallas_call(
        _ffn_kernel,
        grid=(tokens // fblk,),
        in_specs=[pl.BlockSpec((fblk, D_MODEL), lambda i: (i, 0))] + [_resident(a.shape) for a in ffn_in[1:]],
        out_specs=pl.BlockSpec((fblk, D_MODEL), lambda i: (i, 0)),
        out_shape=jax.ShapeDtypeStruct((tokens, D_MODEL), F32),
        compiler_params=pltpu.CompilerParams(
            dimension_semantics=("arbitrary",), vmem_limit_bytes=VMEM_LIMIT),
        name="ffn",
    )(*ffn_in)
    return out.reshape(x.shape)


def kernel(x, norm1_w, w_in, conv_w, conv_b, dt_bias, a_log, d_skip, ssd_norm_w, ret_norm_w,
           w_out, norm2_w, w_gate, w_up, w_down, final_norm_w):
    depth = w_in.shape[0]
    assert depth == 1
    return _layer(x, norm1_w[0], w_in[0], conv_w[0], conv_b[0], dt_bias[0], a_log[0], d_skip[0],
                  ssd_norm_w[0], ret_norm_w[0], w_out[0], norm2_w[0], w_gate[0], w_up[0],
                  w_down[0], final_norm_w)
```

```python
import functools
import math

import numpy as np
import jax
import jax.numpy as jnp
from jax import lax
from jax.experimental import pallas as pl
from jax.experimental.pallas import tpu as pltpu

F32 = jnp.float32
BF16 = jnp.bfloat16

D_MODEL = 1024
SSD_WIDTH = 1024
SSD_HEAD_DIM = 64
SSD_HEADS = 16
SSD_GROUPS = 2
SSD_HPG = SSD_HEADS // SSD_GROUPS
SSD_STATE = 128
GROUP_WIDTH = SSD_WIDTH // SSD_GROUPS
CONV_WIDTH = 4
BC_WIDTH = SSD_GROUPS * SSD_STATE
CONV_CH = SSD_WIDTH + 2 * BC_WIDTH
RET_WIDTH = 1024
RET_HEADS = 8
RET_PAIRS = RET_HEADS // 2
RET_V_DIM = 128
RET_QK_DIM = 64
QK_WIDTH = RET_HEADS * RET_QK_DIM
D_MIX = SSD_WIDTH + RET_WIDTH
CHUNK = 128
D_FF = 2816
ROPE_BASE = 10000.0
EPS = 1e-6
LOG2_E = math.log2(math.e)
LANES = 128
SUBLANES = 8
DT_PAD = LANES
ROT_HALF = RET_QK_DIM // 2
ROT_PARTNER = 2 * ROT_HALF

PROJ_SIZES = [SSD_WIDTH, CONV_CH, SSD_HEADS, QK_WIDTH, QK_WIDTH, RET_WIDTH, RET_WIDTH]
PROJ_OFFS = [int(v) for v in np.cumsum([0] + PROJ_SIZES)]
_WIN_SIZES = [SSD_WIDTH, CONV_CH, DT_PAD, QK_WIDTH, QK_WIDTH, RET_WIDTH, RET_WIDTH]
WIN_OFFS = dict(zip(("z", "xbc", "dt", "q", "k", "v", "g"),
                    (int(v) for v in np.cumsum([0] + _WIN_SIZES[:-1]))))
WIN_WIDTH = sum(_WIN_SIZES)

MIX_BLOCK = 256
SET_REFS = 8
FFN_BLOCK = 512
PROJ_TILE = 256
CONV_TILE = 128
VMEM_LIMIT = 56 * 1024 * 1024

LOG_GAMMA = [math.log1p(-2.0 ** (-5.0 - h)) for h in range(RET_HEADS)]


def _dot(a, b):
    return jnp.dot(a.astype(BF16), b.astype(BF16), preferred_element_type=F32)


def _dot_nt(a, b):
    return lax.dot_general(a.astype(BF16), b.astype(BF16), (((1,), (1,)), ((), ())),
                           preferred_element_type=F32)


def _dot_tn(a, b):
    return lax.dot_general(a.astype(BF16), b.astype(BF16), (((0,), (0,)), ((), ())),
                           preferred_element_type=F32)


def _split(x, terms):
    parts = []
    r = x
    for i in range(terms):
        p = r.astype(BF16)
        parts.append(p)
        if i + 1 < terms:
            r = r - p.astype(F32)
    return jnp.concatenate(parts, axis=1)


def _silu(x):
    return x / (1.0 + jnp.exp(-x))


def _rmsnorm(x, w):
    return x * lax.rsqrt(jnp.mean(x * x, axis=-1, keepdims=True) + EPS) * w


class _Set:
    def __init__(self, refs):
        self.x, self.z, self.xa, self.dt, self.q, self.k, self.v, self.g = refs


def _mixer_kernel(nblk_seq, nblk_total,
                  x_ref, n1w_ref, win_ref,
                  convw_ref, convb_ref, dtb_ref, alog_ref, dskip_ref, ssdnw_ref, retnw_ref,
                  wout_ref, cos_ref, sin_ref,
                  out_ref, *scratch):
    set_a = _Set(scratch[0:SET_REFS])
    set_b = _Set(scratch[SET_REFS:2 * SET_REFS])
    (xbc_s, h_s, y_s, mixed_s, ssd_state, ret_state, tri_s, e2_s, qg_s, kg_s) = scratch[2 * SET_REFS:]
    blk = x_ref.shape[0]
    n_chunks = blk // CHUNK
    t = pl.program_id(0)

    @pl.when(t == 0)
    def _build_constants():
        row = lax.broadcasted_iota(jnp.int32, (CHUNK, CHUNK), 0)
        col = lax.broadcasted_iota(jnp.int32, (CHUNK, CHUNK), 1)
        tri_s[...] = jnp.where(col <= row, 1.0, 0.0).astype(BF16)
        er = lax.broadcasted_iota(jnp.int32, (2 * DT_PAD, SSD_WIDTH), 0) % DT_PAD
        ec = lax.broadcasted_iota(jnp.int32, (2 * DT_PAD, SSD_WIDTH), 1) // SSD_HEAD_DIM
        e2_s[...] = jnp.where(er == ec, 1.0, 0.0).astype(BF16)
        pos = lax.broadcasted_iota(jnp.int32, (CHUNK, LANES), 0).astype(F32)
        lane_head = (lax.broadcasted_iota(jnp.int32, (CHUNK, LANES), 1) // ROT_HALF) % 2
        for p in range(RET_PAIRS):
            lg = jnp.where(lane_head == 0, LOG_GAMMA[2 * p], LOG_GAMMA[2 * p + 1])
            qg_s[p] = jnp.exp(pos * lg)
            kgam = jnp.exp(-pos * lg) * (RET_QK_DIM ** -0.5)
            for hh in range(2):
                kg_s[2 * p + hh] = jnp.where(lane_head == hh, kgam, 0.0)

    @pl.when(jnp.logical_and(t % nblk_seq == 0, t < nblk_total))
    def _reset_conv_carry():
        xbc_s[0:SUBLANES, :] = jnp.zeros((SUBLANES, CONV_CH), F32)

    @pl.when(t == 0)
    def _zero_first_scan_input():
        for ref in scratch[SET_REFS:2 * SET_REFS]:
            ref[...] = jnp.zeros(ref.shape, ref.dtype)

    @pl.when(jnp.logical_or(t == 0, (t + nblk_seq - 1) % nblk_seq == 0))
    def _reset_state():
        ssd_state[...] = jnp.zeros(ssd_state.shape, F32)
        ret_state[...] = jnp.zeros(ret_state.shape, F32)

    def project_units(dst):
        def norm():
            x = x_ref[...]
            dst.x[...] = x
            h_s[...] = _rmsnorm(x, n1w_ref[...]).astype(BF16)

        def proj(base, store, c0, c1):
            store(c0, c1, jnp.dot(h_s[...], win_ref[:, base + c0:base + c1], preferred_element_type=F32))

        def store_to(ref, row0=0):
            def store(c0, c1, val):
                ref[row0:row0 + blk, c0:c1] = val.astype(ref.dtype)
            return store

        def conv(c0, c1):
            u = xbc_s[0:SUBLANES + blk, c0:c1]
            u1 = pltpu.roll(u, 1, axis=0)
            w = [convw_ref[tap:tap + 1, c0:c1] for tap in range(CONV_WIDTH)]
            near = convb_ref[:, c0:c1] + w[3] * u + w[2] * u1
            far = pltpu.roll(w[1] * u + w[0] * u1, 2, axis=0)
            dst.xa[:, c0:c1] = _silu((near + far)[SUBLANES:, :])

        def carry():
            xbc_s[0:SUBLANES, :] = xbc_s[blk:blk + SUBLANES, :]

        def tiles(width, step):
            return [(c, c + step) for c in range(0, width, step)]

        w_off = WIN_OFFS
        conv_in_dots = [functools.partial(proj, w_off["xbc"], store_to(xbc_s, SUBLANES), *cc)
                        for cc in tiles(CONV_CH, PROJ_TILE)]
        other_dots = []
        for name, ref in (("z", dst.z), ("dt", dst.dt), ("q", dst.q), ("k", dst.k),
                          ("v", dst.v), ("g", dst.g)):
            width = ref.shape[1]
            other_dots += [functools.partial(proj, w_off[name], store_to(ref), *cc)
                           for cc in tiles(width, min(width, PROJ_TILE))]
        convs = [functools.partial(conv, *cc) for cc in tiles(CONV_CH, CONV_TILE)]
        return norm, conv_in_dots, other_dots, convs + [carry]

    a_neg2 = -jnp.exp(alog_ref[...]) * LOG2_E
    lane = lax.broadcasted_iota(jnp.int32, (1, LANES), 1)
    lane_ok = lane < SSD_HEADS
    low_half = lane < SSD_HEAD_DIM
    row_i = lax.broadcasted_iota(jnp.int32, (CHUNK, CHUNK), 0)
    col_i = lax.broadcasted_iota(jnp.int32, (CHUNK, CHUNK), 1)
    causal = col_i <= row_i

    def chunk_units(src, c):
        rows = slice(c * CHUNK, (c + 1) * CHUNK)
        env = {}

        def ssd_prep():
            dt_pre = src.dt[rows, :] + dtb_ref[...]
            dt = jnp.maximum(dt_pre, 0.0) + jnp.log1p(jnp.exp(-jnp.abs(dt_pre)))
            dt = jnp.where(lane_ok, dt, 0.0)
            dta = dt * a_neg2
            cs3 = jnp.dot(tri_s[...], _split(dta, 3), preferred_element_type=F32)
            cs = cs3[:, 0:DT_PAD] + cs3[:, DT_PAD:2 * DT_PAD] + cs3[:, 2 * DT_PAD:3 * DT_PAD]
            env["cs"] = cs
            env["csd_t"] = (cs - jnp.log2(dt)).T
            cs_last = cs[CHUNK - 1:CHUNK, :]
            ecs_e = jnp.dot(_split(jnp.exp2(cs), 2), e2_s[...], preferred_element_type=F32)
            dtd_e = jnp.dot(_split(dt * jnp.exp2(cs_last - cs), 2), e2_s[...], preferred_element_type=F32)
            env["ecs_e"] = ecs_e
            env["chunk_decay"] = ecs_e[CHUNK - 1:CHUNK, :]
            env["xdtd_b"] = (src.xa[rows, 0:SSD_WIDTH] * dtd_e).astype(BF16)

        def ssd_group(g):
            gs = slice(g * GROUP_WIDTH, (g + 1) * GROUP_WIDTH)
            bg = src.xa[rows, SSD_WIDTH + g * SSD_STATE:SSD_WIDTH + (g + 1) * SSD_STATE].astype(BF16)
            cg = src.xa[rows, SSD_WIDTH + BC_WIDTH + g * SSD_STATE:
                        SSD_WIDTH + BC_WIDTH + (g + 1) * SSD_STATE].astype(BF16)
            env["cb"] = _dot_nt(cg, bg)
            prev = ssd_state[g]
            env["y_off"] = _dot(cg, prev) * env["ecs_e"][:, gs]
            ssd_state[g] = prev * env["chunk_decay"][:, gs] + _dot_tn(bg, env["xdtd_b"][:, gs])

        def ssd_tile(g, tl):
            ts = slice(g * GROUP_WIDTH + tl * LANES, g * GROUP_WIDTH + (tl + 1) * LANES)
            cs, csd_t, cb = env["cs"], env["csd_t"], env["cb"]
            ms = []
            for hh in range(2):
                hd = g * SSD_HPG + 2 * tl + hh
                seg = cs[:, hd:hd + 1] - csd_t[hd:hd + 1, :]
                decay_dt = jnp.exp2(jnp.where(causal, seg, -jnp.inf))
                ms.append((cb * decay_dt).astype(BF16))
            xt = src.xa[rows, ts].astype(BF16)
            zero = jnp.zeros_like(xt)
            rhs = jnp.concatenate([jnp.where(low_half, xt, zero), jnp.where(low_half, zero, xt)], axis=0)
            y = jnp.dot(jnp.concatenate(ms, axis=1), rhs, preferred_element_type=F32)
            y = y + env["y_off"][:, tl * LANES:(tl + 1) * LANES] + src.xa[rows, ts] * dskip_ref[:, ts]
            y_s[:, ts] = y * _silu(src.z[rows, ts])

        def ssd_norm(g):
            gs = slice(g * GROUP_WIDTH, (g + 1) * GROUP_WIDTH)
            yg = y_s[:, gs]
            yg = yg * lax.rsqrt(jnp.mean(yg * yg, axis=-1, keepdims=True) + EPS)
            mixed_s[rows, gs] = (yg * ssdnw_ref[:, gs]).astype(BF16)

        def ret_pair(p):
            ts = slice(p * LANES, (p + 1) * LANES)
            cos = cos_ref[rows, :]
            sin = sin_ref[rows, :]
            qt = src.q[rows, ts]
            kt = src.k[rows, ts]
            qg = ((qt * cos + pltpu.roll(qt, ROT_PARTNER, axis=1) * sin) * qg_s[p]).astype(BF16)
            kr = kt * cos + pltpu.roll(kt, ROT_PARTNER, axis=1) * sin
            kg = [(kr * kg_s[2 * p + hh]).astype(BF16) for hh in range(2)]
            env["qg"], env["kg"] = qg, kg
            env["scores"] = _dot_nt(qg, jnp.concatenate(kg, axis=0))

        def ret_head(p, hh):
            hd = 2 * p + hh
            qg, kg, scores = env["qg"], env["kg"], env["scores"]
            vs = slice(hd * RET_V_DIM, (hd + 1) * RET_V_DIM)
            v_h = src.v[rows, vs]
            probs = jnp.where(causal, scores[:, hh * CHUNK:(hh + 1) * CHUNK], 0.0).astype(BF16)
            u_prev = ret_state[hd]
            yr = jnp.dot(jnp.concatenate([probs, qg], axis=1),
                         jnp.concatenate([v_h, u_prev.astype(BF16)], axis=0),
                         preferred_element_type=F32)
            ret_state[hd] = (u_prev + _dot_tn(kg[hh], v_h)) * math.exp(CHUNK * LOG_GAMMA[hd])
            mu = jnp.mean(yr, axis=-1, keepdims=True)
            yc = yr - mu
            var = jnp.mean(yc * yc, axis=-1, keepdims=True)
            yn = yc * lax.rsqrt(var + EPS)
            yn = yn * retnw_ref[:, vs] * _silu(src.g[rows, vs])
            mixed_s[rows, SSD_WIDTH + hd * RET_V_DIM:SSD_WIDTH + (hd + 1) * RET_V_DIM] = yn.astype(BF16)

        ssd = []
        for g in range(SSD_GROUPS):
            ssd.append(functools.partial(ssd_group, g))
            ssd += [functools.partial(ssd_tile, g, tl) for tl in range(GROUP_WIDTH // LANES)]
            ssd.append(functools.partial(ssd_norm, g))
        ret = []
        for p in range(RET_PAIRS):
            ret.append(functools.partial(ret_pair, p))
            ret += [functools.partial(ret_head, p, hh) for hh in range(2)]
        return ssd_prep, ssd, ret

    def scan_units(src):
        chunks = [chunk_units(src, c) for c in range(n_chunks)]
        units = [prep for prep, _, _ in chunks]
        for c, (_, ssd, ret) in enumerate(chunks):
            units += interleave(ssd, ret)
            if c + 1 < n_chunks:
                units += out_proj_units(src, c)
        return units

    def out_proj_units(src, c):
        rows = slice(c * CHUNK, (c + 1) * CHUNK)

        def out_proj(c0, c1):
            out_ref[rows, c0:c1] = src.x[rows, c0:c1] + jnp.dot(mixed_s[rows, :], wout_ref[:, c0:c1],
                                                                 preferred_element_type=F32)
        return [functools.partial(out_proj, c0, c0 + PROJ_TILE) for c0 in range(0, D_MODEL, PROJ_TILE)]

    def interleave(a, b):
        na, nb = len(a), len(b)
        merged = []
        ia = ib = 0
        while ia < na or ib < nb:
            if ib >= nb or (ia < na and ia * nb <= ib * na):
                merged.append(a[ia])
                ia += 1
            else:
                merged.append(b[ib])
                ib += 1
        return merged

    def step(src, dst):
        norm, conv_in_dots, other_dots, convs = project_units(dst)
        norm()
        for unit in (interleave(scan_units(src), conv_in_dots + other_dots)
                     + interleave(convs, out_proj_units(src, n_chunks - 1))):
            unit()

    @pl.when(t % 2 == 1)
    def _odd():
        step(set_a, set_b)

    @pl.when(t % 2 == 0)
    def _even():
        step(set_b, set_a)


def _ffn_kernel(x_ref, n2w_ref, wg_ref, wu_ref, wd_ref, fnw_ref, out_ref):
    x = x_ref[...]
    h = _rmsnorm(x, n2w_ref[...]).astype(BF16)
    gate = jnp.dot(h, wg_ref[...], preferred_element_type=F32)
    up = jnp.dot(h, wu_ref[...], preferred_element_type=F32)
    act = (_silu(gate) * up).astype(BF16)
    y = x + jnp.dot(act, wd_ref[...], preferred_element_type=F32)
    out_ref[...] = _rmsnorm(y, fnw_ref[...])


def _resident(shape):
    nd = len(shape)
    return pl.BlockSpec(shape, lambda *_: (0,) * nd, pipeline_mode=pl.Buffered(1))


def _qk_perm():
    half = RET_QK_DIM // 2
    idx = []
    for p in range(RET_PAIRS):
        for part in range(2):
            for hh in range(2):
                base = (2 * p + hh) * RET_QK_DIM + part * half
                idx.extend(range(base, base + half))
    return np.asarray(idx, dtype=np.int32)


def _rope_tables(seqlen):
    half = RET_QK_DIM // 2
    inv_freq = ROPE_BASE ** (-np.arange(half, dtype=np.float64) / half)
    ang = np.arange(seqlen, dtype=np.float64)[:, None] * inv_freq[None, :]
    cos = np.cos(ang)
    sin = np.sin(ang)
    return (jnp.asarray(np.concatenate([cos, cos, cos, cos], axis=1), dtype=F32),
            jnp.asarray(np.concatenate([-sin, -sin, sin, sin], axis=1), dtype=F32))


def _layer(x, norm1_w, w_in, conv_w, conv_b, dt_bias, a_log, d_skip, ssd_norm_w,
           ret_norm_w, w_out, norm2_w, w_gate, w_up, w_down, out_norm_w):
    bsz, seqlen, _ = x.shape
    tokens = bsz * seqlen
    blk = MIX_BLOCK
    nblk_seq = seqlen // blk
    nblk_total = tokens // blk
    o = PROJ_OFFS
    perm = _qk_perm()
    w_all = jnp.concatenate([
        w_in[:, o[0]:o[1]], w_in[:, o[1]:o[2]],
        jnp.pad(w_in[:, o[2]:o[3]], ((0, 0), (0, DT_PAD - SSD_HEADS))),
        w_in[:, o[3]:o[4]][:, perm], w_in[:, o[4]:o[5]][:, perm],
        w_in[:, o[5]:o[6]], w_in[:, o[6]:o[7]]], axis=1).astype(BF16)
    assert w_all.shape[1] == WIN_WIDTH
    pad16 = lambda v: jnp.pad(v.astype(F32)[None, :], ((0, 0), (0, DT_PAD - SSD_HEADS)))
    cos_t, sin_t = _rope_tables(seqlen)
    row = lambda v: v.astype(F32)[None, :]

    mixer_in = [
        x.reshape(tokens, D_MODEL), row(norm1_w), w_all,
        conv_w.astype(F32), row(conv_b), pad16(dt_bias), pad16(a_log),
        row(jnp.repeat(d_skip, SSD_HEAD_DIM)), row(ssd_norm_w), row(ret_norm_w),
        w_out.astype(BF16), cos_t, sin_t,
    ]
    last = nblk_total - 1
    in_specs = [pl.BlockSpec((blk, D_MODEL), lambda t: (jnp.minimum(t, last), 0))]
    in_specs += [_resident(a.shape) for a in mixer_in[1:-2]]
    rope_spec = pl.BlockSpec((blk, LANES), lambda t: (jnp.clip(t - 1, 0, last) % nblk_seq, 0))
    in_specs += [rope_spec, rope_spec]
    act_set = [
        pltpu.VMEM((blk, D_MODEL), F32),
        pltpu.VMEM((blk, SSD_WIDTH), F32),
        pltpu.VMEM((blk, CONV_CH), F32),
        pltpu.VMEM((blk, DT_PAD), F32),
        pltpu.VMEM((blk, QK_WIDTH), F32),
        pltpu.VMEM((blk, QK_WIDTH), F32),
        pltpu.VMEM((blk, RET_WIDTH), BF16),
        pltpu.VMEM((blk, RET_WIDTH), F32),
    ]
    assert len(act_set) == SET_REFS
    scratch = act_set + act_set + [
        pltpu.VMEM((blk + 2 * SUBLANES, CONV_CH), F32),
        pltpu.VMEM((blk, D_MODEL), BF16),
        pltpu.VMEM((CHUNK, SSD_WIDTH), F32),
        pltpu.VMEM((blk, D_MIX), BF16),
        pltpu.VMEM((SSD_GROUPS, SSD_STATE, GROUP_WIDTH), F32),
        pltpu.VMEM((RET_HEADS, LANES, RET_V_DIM), F32),
        pltpu.VMEM((CHUNK, CHUNK), BF16),
        pltpu.VMEM((2 * DT_PAD, SSD_WIDTH), BF16),
        pltpu.VMEM((RET_PAIRS, CHUNK, LANES), F32),
        pltpu.VMEM((RET_HEADS, CHUNK, LANES), F32),
    ]
    x1 = pl.pallas_call(
        functools.partial(_mixer_kernel, nblk_seq, nblk_total),
        grid=(nblk_total + 1,),
        in_specs=in_specs,
        out_specs=pl.BlockSpec((blk, D_MODEL), lambda t: (jnp.maximum(t - 1, 0), 0)),
        out_shape=jax.ShapeDtypeStruct((tokens, D_MODEL), F32),
        scratch_shapes=scratch,
        compiler_params=pltpu.CompilerParams(
            dimension_semantics=("arbitrary",), vmem_limit_bytes=VMEM_LIMIT),
        name="mixer",
    )(*mixer_in)

    fblk = FFN_BLOCK
    ffn_in = [x1, row(norm2_w), w_gate.astype(BF16), w_up.astype(BF16),
              w_down.astype(BF16), row(out_norm_w)]
    out = pl.pallas_call(
        _ffn_kernel,
        grid=(tokens // fblk,),
        in_specs=[pl.BlockSpec((fblk, D_MODEL), lambda i: (i, 0))] + [_resident(a.shape) for a in ffn_in[1:]],
        out_specs=pl.BlockSpec((fblk, D_MODEL), lambda i: (i, 0)),
        out_shape=jax.ShapeDtypeStruct((tokens, D_MODEL), F32),
        compiler_params=pltpu.CompilerParams(
            dimension_semantics=("arbitrary",), vmem_limit_bytes=VMEM_LIMIT),
        name="ffn",
    )(*ffn_in)
    return out.reshape(x.shape)


def kernel(x, norm1_w, w_in, conv_w, conv_b, dt_bias, a_log, d_skip, ssd_norm_w, ret_norm_w,
           w_out, norm2_w, w_gate, w_up, w_down, final_norm_w):
    depth = w_in.shape[0]
    assert depth == 1
    return _layer(x, norm1_w[0], w_in[0], conv_w[0], conv_b[0], dt_bias[0], a_log[0], d_skip[0],
                  ssd_norm_w[0], ret_norm_w[0], w_out[0], norm2_w[0], w_gate[0], w_up[0],
                  w_down[0], final_norm_w)
```
